```python
import jax, jax.numpy as jnp
from jax import lax
import numpy as np

D_MODEL = 2048
BATCH = 1
SEQ = 16384
DEPTH = 2

D_A = D_MODEL // 2
CONF_KERNEL = 31
D_B = D_MODEL // 2
FNET_GROUPS = 8
FNET_GROUP_CH = D_B // FNET_GROUPS
IN_EVEN = 2 * D_A + D_B
MIX_EVEN = D_A + D_B

D_C = D_MODEL // 2
SHORT_KERNEL = 3
MLA_HEADS = 8
Q_LORA = 512
KV_LORA = 256
QK_NOPE = 128
QK_ROPE = 64
V_HEAD = 128
QK_HEAD = QK_NOPE + QK_ROPE
D_ATT = MLA_HEADS * V_HEAD
IN_ODD = 3 * D_C + Q_LORA + KV_LORA + QK_ROPE
MIX_ODD = D_C + D_ATT
ROPE_THETA = 10000.0
Q_BLOCK = 128

D_FF = 4 * D_MODEL
EPS = 1e-6

kernel_name = "hybrid_conformer_fnet_shortconv_mla_encoder"


def rmsnorm(x, g):
    xf = x.astype(jnp.float32)
    y = xf * lax.rsqrt(jnp.mean(xf * xf, axis=-1, keepdims=True) + EPS)
    return (y * g.astype(jnp.float32)).astype(x.dtype)


def layernorm(x, g, b):
    xf = x.astype(jnp.float32)
    mu = jnp.mean(xf, axis=-1, keepdims=True)
    var = jnp.mean(jnp.square(xf - mu), axis=-1, keepdims=True)
    y = (xf - mu) * lax.rsqrt(var + EPS)
    return (y * g.astype(jnp.float32) + b.astype(jnp.float32)).astype(x.dtype)


def dwconv(x, w):
    k, c = w.shape
    pad = k // 2
    return lax.conv_general_dilated(
        x, w[:, None, :].astype(x.dtype), window_strides=(1,),
        padding=[(pad, pad)], dimension_numbers=("NWC", "WIO", "NWC"),
        feature_group_count=c)


def conformer_conv(u, conv_w, conv_b, ln_g, ln_b):
    val, gate = u[..., :D_A], u[..., D_A:]
    h = val * jax.nn.sigmoid(gate)
    h = dwconv(h, conv_w) + conv_b.astype(h.dtype)
    h = layernorm(h, ln_g, ln_b)
    return jax.nn.silu(h)


def fourier_mix(u):
    b, s, _ = u.shape
    z = u.astype(jnp.float32).reshape(b, s, FNET_GROUPS, FNET_GROUP_CH)
    f = jnp.fft.fftn(z, axes=(1, 3), norm="ortho")
    return jnp.real(f).reshape(b, s, D_B).astype(u.dtype)


def short_gated_conv(u, conv_w):
    b_gate = u[..., :D_C]
    c_gate = u[..., D_C:2 * D_C]
    h = u[..., 2 * D_C:]
    return b_gate * dwconv(c_gate * h, conv_w)


def rope_tables(seq):
    inv = 1.0 / (ROPE_THETA ** (jnp.arange(0, QK_ROPE, 2, dtype=jnp.float32) / QK_ROPE))
    ang = jnp.arange(seq, dtype=jnp.float32)[:, None] * inv[None, :]
    return jnp.cos(ang), jnp.sin(ang)


def apply_rope(x, cos, sin):
    half = x.shape[-1] // 2
    x1, x2 = x[..., :half], x[..., half:]
    cos = cos.astype(x.dtype)
    sin = sin.astype(x.dtype)
    return jnp.concatenate([x1 * cos - x2 * sin, x1 * sin + x2 * cos], axis=-1)


def mla(u, q_norm_g, w_uq, kv_norm_g, w_ukv, cos, sin):
    b, s, _ = u.shape
    c_q = u[..., :Q_LORA]
    c_kv = u[..., Q_LORA:Q_LORA + KV_LORA]
    k_r = u[..., Q_LORA + KV_LORA:]
    q = jnp.einsum('bsr,re->bse', rmsnorm(c_q, q_norm_g), w_uq)
    q = q.reshape(b, s, MLA_HEADS, QK_HEAD)
    q = jnp.concatenate([q[..., :QK_NOPE],
                         apply_rope(q[..., QK_NOPE:], cos[None, :, None, :], sin[None, :, None, :])], axis=-1)
    kv = jnp.einsum('bsr,re->bse', rmsnorm(c_kv, kv_norm_g), w_ukv)
    kv = kv.reshape(b, s, MLA_HEADS, QK_NOPE + V_HEAD)
    k_nope, v = kv[..., :QK_NOPE], kv[..., QK_NOPE:]
    k_r = apply_rope(k_r, cos[None], sin[None])
    k = jnp.concatenate([k_nope, jnp.broadcast_to(k_r[:, :, None, :], (b, s, MLA_HEADS, QK_ROPE))], axis=-1)
    scale = QK_HEAD ** -0.5
    nblk = s // Q_BLOCK
    qb = q.reshape(b, nblk, Q_BLOCK, MLA_HEADS, QK_HEAD).transpose(1, 0, 2, 3, 4)

    def attend(qblk):
        sc = jnp.einsum('bqhd,bkhd->bhqk', qblk, k).astype(jnp.float32) * scale
        p = jax.nn.softmax(sc, axis=-1).astype(v.dtype)
        return jnp.einsum('bhqk,bkhd->bqhd', p, v)

    o = lax.map(attend, qb)
    return o.transpose(1, 0, 2, 3, 4).reshape(b, s, D_ATT)


def sq_relu_mlp(h, w_up, w_down):
    a = jnp.einsum('bsd,df->bsf', h, w_up)
    a = jnp.square(jax.nn.relu(a))
    return jnp.einsum('bsf,fd->bsd', a, w_down)


def setup_inputs(seed: int = 0) -> dict:
    key = jax.random.key(seed)
    ks = iter(jax.random.split(key, 32))
    n_even = (DEPTH + 1) // 2
    n_odd = DEPTH // 2

    def w(shape, fan_in):
        return jax.random.normal(next(ks), shape, jnp.float32) * (fan_in ** -0.5)

    def gain(shape):
        return 1.0 + 0.01 * jax.random.normal(next(ks), shape, jnp.float32)

    def bias(shape):
        return 0.02 * jax.random.normal(next(ks), shape, jnp.float32)

    return {
        "x": jax.random.normal(next(ks), (BATCH, SEQ, D_MODEL), jnp.float32),
        "mix_norm_e": gain((n_even, D_MODEL)),
        "w_in_e": w((n_even, D_MODEL, IN_EVEN), D_MODEL),
        "conv_a_w": w((n_even, CONF_KERNEL, D_A), CONF_KERNEL),
        "conv_a_b": bias((n_even, D_A)),
        "ln_a_g": gain((n_even, D_A)),
        "ln_a_b": bias((n_even, D_A)),
        "w_out_e": w((n_even, MIX_EVEN, D_MODEL), MIX_EVEN),
        "mix_norm_o": gain((n_odd, D_MODEL)),
        "w_in_o": w((n_odd, D_MODEL, IN_ODD), D_MODEL),
        "conv_c_w": w((n_odd, SHORT_KERNEL, D_C), SHORT_KERNEL),
        "q_norm_g": gain((n_odd, Q_LORA)),
        "w_uq": w((n_odd, Q_LORA, MLA_HEADS * QK_HEAD), Q_LORA),
        "kv_norm_g": gain((n_odd, KV_LORA)),
        "w_ukv": w((n_odd, KV_LORA, MLA_HEADS * (QK_NOPE + V_HEAD)), KV_LORA),
        "w_out_o": w((n_odd, MIX_ODD, D_MODEL), MIX_ODD),
        "mlp_norm": gain((DEPTH, D_MODEL)),
        "w_up": w((DEPTH, D_MODEL, D_FF), D_MODEL),
        "w_down": w((DEPTH, D_FF, D_MODEL), D_FF),
        "final_norm": gain((D_MODEL,)),
    }


def reference(x, mix_norm_e, w_in_e, conv_a_w, conv_a_b, ln_a_g, ln_a_b, w_out_e,
              mix_norm_o, w_in_o, conv_c_w, q_norm_g, w_uq, kv_norm_g, w_ukv, w_out_o,
              mlp_norm, w_up, w_down, final_norm):
    cos, sin = rope_tables(x.shape[1])
    for i in range(DEPTH):
        j = i // 2
        if i % 2 == 0:
            h = rmsnorm(x, mix_norm_e[j])
            u = jnp.einsum('bsd,de->bse', h, w_in_e[j])
            ya = conformer_conv(u[..., :2 * D_A], conv_a_w[j], conv_a_b[j], ln_a_g[j], ln_a_b[j])
            yb = fourier_mix(u[..., 2 * D_A:])
            y = jnp.concatenate([ya, yb], axis=-1)
            x = x + jnp.einsum('bse,ed->bsd', y, w_out_e[j])
        else:
            h = rmsnorm(x, mix_norm_o[j])
            u = jnp.einsum('bsd,de->bse', h, w_in_o[j])
            yc = short_gated_conv(u[..., :3 * D_C], conv_c_w[j])
            yd = mla(u[..., 3 * D_C:], q_norm_g[j], w_uq[j], kv_norm_g[j], w_ukv[j], cos, sin)
            y = jnp.concatenate([yc, yd], axis=-1)
            x = x + jnp.einsum('bse,ed->bsd', y, w_out_o[j])
        x = x + sq_relu_mlp(rmsnorm(x, mlp_norm[i]), w_up[i], w_down[i])
    return rmsnorm(x, final_norm)
```

```python
import functools
import math

import numpy as np
import jax
import jax.numpy as jnp
from jax import lax
from jax.experimental import pallas as pl
from jax.experimental.pallas import tpu as pltpu

D_MODEL = 2048
SEQ = 16384
D_A = 1024
CONF_KERNEL = 31
D_B = 1024
FNET_GROUPS = 8
FNET_GROUP_CH = 128
D_C = 1024
MLA_HEADS = 8
Q_LORA = 512
KV_LORA = 256
QK_NOPE = 128
QK_ROPE = 64
V_HEAD = 128
QK_HEAD = QK_NOPE + QK_ROPE
D_ATT = MLA_HEADS * V_HEAD
ROPE_THETA = 10000.0
D_FF = 4 * D_MODEL
EPS = 1e-6

LANES = 128
QK_PAD = 2 * LANES
FFT_N1 = 128
FFT_N2 = 128
VMEM_LIMIT = 56 * 1024 * 1024

BF16 = jnp.bfloat16
F32 = jnp.float32


def _dot(a, b):
    return jnp.dot(a, b, preferred_element_type=F32)


def _rms(x, g):
    return x * lax.rsqrt(jnp.mean(x * x, axis=-1, keepdims=True) + EPS) * g


def _params(sem):
    return pltpu.CompilerParams(dimension_semantics=sem, vmem_limit_bytes=VMEM_LIMIT)


def _resident(shape):
    nd = len(shape)
    return pl.BlockSpec(shape, lambda *_: (0,) * nd, pipeline_mode=pl.Buffered(1))


TM_IN = 512


def _inproj_e_kernel(x_ref, g_ref, w_ref, glu_ref, uf_ref):
    h = _rms(x_ref[...], g_ref[...]).astype(BF16)
    for c in range(0, D_A, 512):
        val = _dot(h, w_ref[:, c:c + 512])
        gate = _dot(h, w_ref[:, D_A + c:D_A + c + 512])
        glu_ref[:, c:c + 512] = val * jax.nn.sigmoid(gate)
    for c in range(0, D_B, 512):
        uf_ref[:, c:c + 512] = _dot(h, w_ref[:, 2 * D_A + c:2 * D_A + c + 512]).astype(BF16)


def _inproj_e(x, g, w):
    return pl.pallas_call(
        _inproj_e_kernel,
        grid=(SEQ // TM_IN,),
        in_specs=[pl.BlockSpec((TM_IN, D_MODEL), lambda i: (i, 0)),
                  _resident((1, D_MODEL)),
                  _resident((D_MODEL, 2 * D_A + D_B))],
        out_specs=[pl.BlockSpec((TM_IN, D_A), lambda i: (i, 0)),
                   pl.BlockSpec((TM_IN, D_B), lambda i: (i, 0))],
        out_shape=[jax.ShapeDtypeStruct((SEQ, D_A), F32),
                   jax.ShapeDtypeStruct((SEQ, D_B), BF16)],
        compiler_params=_params(("arbitrary",)),
        name="inproj_e",
    )(x, g, w)


TC_CONF = 128
HALO_CONF = 16


def _conformer_kernel(prev_ref, main_ref, next_ref, w_ref, b_ref, lg_ref, lb_ref, o_ref,
                      hext_ref, conv_ref):
    i = pl.program_id(0)
    n = pl.num_programs(0)
    hext_ref[0:HALO_CONF, :] = jnp.where(i > 0, prev_ref[...], 0.0)
    hext_ref[HALO_CONF:HALO_CONF + TC_CONF, :] = main_ref[...]
    hext_ref[HALO_CONF + TC_CONF:, :] = jnp.where(i < n - 1, next_ref[...], 0.0)
    base = HALO_CONF - CONF_KERNEL // 2
    for c in range(0, D_A, LANES):
        acc = jnp.zeros((TC_CONF, LANES), F32) + b_ref[:, c:c + LANES]
        for t in range(CONF_KERNEL):
            acc = acc + w_ref[t:t + 1, c:c + LANES] * hext_ref[base + t:base + t + TC_CONF, c:c + LANES]
        conv_ref[:, c:c + LANES] = acc
    y = conv_ref[...]
    mu = jnp.mean(y, axis=-1, keepdims=True)
    d = y - mu
    var = jnp.mean(d * d, axis=-1, keepdims=True)
    z = d * lax.rsqrt(var + EPS) * lg_ref[...] + lb_ref[...]
    o_ref[...] = (z * jax.nn.sigmoid(z)).astype(BF16)


def _conformer(hglu, w, b, lg, lb):
    r = TC_CONF // HALO_CONF
    nh = SEQ // HALO_CONF
    return pl.pallas_call(
        _conformer_kernel,
        grid=(SEQ // TC_CONF,),
        in_specs=[pl.BlockSpec((HALO_CONF, D_A), lambda i: (jnp.maximum(i * r - 1, 0), 0)),
                  pl.BlockSpec((TC_CONF, D_A), lambda i: (i, 0)),
                  pl.BlockSpec((HALO_CONF, D_A), lambda i: (jnp.minimum((i + 1) * r, nh - 1), 0)),
                  _resident((CONF_KERNEL, D_A)),
                  _resident((1, D_A)), _resident((1, D_A)), _resident((1, D_A))],
        out_specs=pl.BlockSpec((TC_CONF, D_A), lambda i: (i, 0)),
        out_shape=jax.ShapeDtypeStruct((SEQ, D_A), BF16),
        scratch_shapes=[pltpu.VMEM((TC_CONF + 2 * HALO_CONF, D_A), F32),
                        pltpu.VMEM((TC_CONF, D_A), F32)],
        compiler_params=_params(("arbitrary",)),
        name="conformer",
    )(hglu, hglu, hglu, w, b, lg, lb)


TN_FFT1 = 8192
TK1_FFT2 = 8


def _fft1_kernel(x_ref, f_ref, tr_ref, ti_ref):
    for c in range(0, TN_FFT1, 1024):
        t = _dot(f_ref[...], x_ref[:, c:c + 1024])
        tr_ref[:, c:c + 1024] = t[:FFT_N1].astype(BF16)
        ti_ref[:, c:c + 1024] = t[FFT_N1:].astype(BF16)


def _fft1(x2d, f_stack):
    ncol = FFT_N2 * D_B
    return pl.pallas_call(
        _fft1_kernel,
        grid=(ncol // TN_FFT1,),
        in_specs=[pl.BlockSpec((FFT_N1, TN_FFT1), lambda j: (0, j)),
                  _resident((2 * FFT_N1, FFT_N1))],
        out_specs=[pl.BlockSpec((FFT_N1, TN_FFT1), lambda j: (0, j)),
                   pl.BlockSpec((FFT_N1, TN_FFT1), lambda j: (0, j))],
        out_shape=[jax.ShapeDtypeStruct((FFT_N1, ncol), BF16),
                   jax.ShapeDtypeStruct((FFT_N1, ncol), BF16)],
        compiler_params=_params(("arbitrary",)),
        name="fft_stage1",
    )(x2d, f_stack)


def _fft2_kernel(tr_ref, ti_ref, m_ref, cs_ref, o_ref):
    for j in range(TK1_FFT2):
        t = jnp.concatenate([tr_ref[j], ti_ref[j]], axis=0)
        z = _dot(m_ref[j], t)
        zr = z[:FFT_N2].astype(BF16)
        zi = z[FFT_N2:].astype(BF16)
        for g in range(FNET_GROUPS):
            lo = g * FNET_GROUP_CH
            zz = jnp.concatenate([zr[:, lo:lo + FNET_GROUP_CH], zi[:, lo:lo + FNET_GROUP_CH]], axis=1)
            o_ref[:, j * D_B + lo:j * D_B + lo + FNET_GROUP_CH] = _dot(zz, cs_ref[...]).astype(BF16)


def _fft2(tr3, ti3, m_full, cs):
    return pl.pallas_call(
        _fft2_kernel,
        grid=(FFT_N1 // TK1_FFT2,),
        in_specs=[pl.BlockSpec((TK1_FFT2, FFT_N2, D_B), lambda i: (i, 0, 0)),
                  pl.BlockSpec((TK1_FFT2, FFT_N2, D_B), lambda i: (i, 0, 0)),
                  pl.BlockSpec((TK1_FFT2, 2 * FFT_N2, 2 * FFT_N2), lambda i: (i, 0, 0)),
                  _resident((2 * FNET_GROUP_CH, FNET_GROUP_CH))],
        out_specs=pl.BlockSpec((FFT_N2, TK1_FFT2 * D_B), lambda i: (0, i)),
        out_shape=jax.ShapeDtypeStruct((FFT_N2, FFT_N1 * D_B), BF16),
        compiler_params=_params(("arbitrary",)),
        name="fft_stage2",
    )(tr3, ti3, m_full, cs)


def _dft_tables():
    n = np.arange(128)
    ph = 2.0 * np.pi * ((n[:, None] * n[None, :]) % 128) / 128.0
    fr, fi = np.cos(ph), -np.sin(ph)
    f_stack = np.concatenate([fr, fi], axis=0) / math.sqrt(SEQ)
    tw_ph = 2.0 * np.pi * (n[:, None] * n[None, :]) / SEQ
    twr, twi = np.cos(tw_ph), -np.sin(tw_ph)
    cs = np.concatenate([np.cos(ph), np.sin(ph)], axis=0) / math.sqrt(FNET_GROUP_CH)
    return (jnp.asarray(f_stack, BF16), jnp.asarray(fr, F32), jnp.asarray(fi, F32),
            jnp.asarray(twr, F32), jnp.asarray(twi, F32), jnp.asarray(cs, BF16))


def _fourier_mix(uf):
    f_stack, fr, fi, twr, twi, cs = _dft_tables()
    mr = fr[None] * twr[:, None, :] - fi[None] * twi[:, None, :]
    mi = fr[None] * twi[:, None, :] + fi[None] * twr[:, None, :]
    m_full = jnp.concatenate([jnp.concatenate([mr, -mi], axis=2),
                              jnp.concatenate([mi, mr], axis=2)], axis=1).astype(BF16)
    tr, ti = _fft1(uf.reshape(FFT_N1, FFT_N2 * D_B), f_stack)
    y2d = _fft2(tr.reshape(FFT_N1, FFT_N2, D_B), ti.reshape(FFT_N1, FFT_N2, D_B), m_full, cs)
    return y2d.reshape(SEQ, D_B)


TM_MLP = 512
TF_MLP = 512


def _mlp_kernel(x_ref, ya_ref, yb_ref, wo_ref, g_ref, wup_ref, wdn_ref, gf_ref, o_ref, h_ref, *,
                final):
    f = pl.program_id(1)
    half = wo_ref.shape[0] // 2

    @pl.when(f == 0)
    def _():
        for c in range(0, D_MODEL, 512):
            o_ref[:, c:c + 512] = (x_ref[:, c:c + 512]
                                   + _dot(ya_ref[...], wo_ref[0:half, c:c + 512])
                                   + _dot(yb_ref[...], wo_ref[half:, c:c + 512]))
        h_ref[...] = _rms(o_ref[...], g_ref[...]).astype(BF16)

    a = _dot(h_ref[...], wup_ref[...])
    a = jnp.square(jnp.maximum(a, 0.0)).astype(BF16)
    for c in range(0, D_MODEL, 512):
        o_ref[:, c:c + 512] += _dot(a, wdn_ref[:, c:c + 512])

    if final:
        @pl.when(f == pl.num_programs(1) - 1)
        def _():
            o_ref[...] = _rms(o_ref[...], gf_ref[...])


def _outproj_mlp(x, ya, yb, wo, g, wup, wdn, gf, final):
    return pl.pallas_call(
        functools.partial(_mlp_kernel, final=final),
        grid=(SEQ // TM_MLP, D_FF // TF_MLP),
        in_specs=[pl.BlockSpec((TM_MLP, D_MODEL), lambda i, f: (i, 0)),
                  pl.BlockSpec((TM_MLP, ya.shape[1]), lambda i, f: (i, 0)),
                  pl.BlockSpec((TM_MLP, yb.shape[1]), lambda i, f: (i, 0)),
                  _resident(wo.shape),
                  _resident((1, D_MODEL)),
                  pl.BlockSpec((D_MODEL, TF_MLP), lambda i, f: (0, f)),
                  pl.BlockSpec((TF_MLP, D_MODEL), lambda i, f: (f, 0)),
                  _resident((1, D_MODEL))],
        out_specs=pl.BlockSpec((TM_MLP, D_MODEL), lambda i, f: (i, 0)),
        out_shape=jax.ShapeDtypeStruct((SEQ, D_MODEL), F32),
        scratch_shapes=[pltpu.VMEM((TM_MLP, D_MODEL), BF16)],
        compiler_params=_params(("arbitrary", "arbitrary")),
        name="outproj_mlp_final" if final else "outproj_mlp",
    )(x, ya, yb, wo, g, wup, wdn, gf)


TM_INO = 256
Q_COLS = 3 * LANES


def _inproj_o_kernel(x_ref, g_ref, wc_ref, wl_ref, qg_ref, wuq_ref, kvg_ref, wukv_ref, cos_ref, sin_ref,
                     bg_ref, ch_ref, q_ref, k_ref, v_ref, *, qscale):
    h = _rms(x_ref[...], g_ref[...]).astype(BF16)
    for c in range(0, D_C, 512):
        bg_ref[:, c:c + 512] = _dot(h, wc_ref[:, c:c + 512])
        ch_ref[:, c:c + 512] = (_dot(h, wc_ref[:, D_C + c:D_C + c + 512])
                                * _dot(h, wc_ref[:, 2 * D_C + c:2 * D_C + c + 512]))
    cosv = cos_ref[...]
    sinv = sin_ref[...]
    lat = _dot(h, wl_ref[...])
    o = Q_LORA + KV_LORA
    kr = (lat[:, o:o + LANES] * cosv + lat[:, o + LANES:o + 2 * LANES] * sinv).astype(BF16)
    cq = _rms(lat[:, :Q_LORA], qg_ref[...]).astype(BF16)
    ckv = _rms(lat[:, Q_LORA:o], kvg_ref[...]).astype(BF16)
    for hd in range(MLA_HEADS):
        qh = _dot(cq, wuq_ref[:, hd * Q_COLS:(hd + 1) * Q_COLS])
        q_ref[hd, :, 0:LANES] = (qh[:, 0:LANES] * qscale).astype(BF16)
        q_ref[hd, :, LANES:] = ((qh[:, LANES:2 * LANES] * cosv + qh[:, 2 * LANES:] * sinv)
                                * qscale).astype(BF16)
        kv = _dot(ckv, wukv_ref[:, hd * 2 * LANES:(hd + 1) * 2 * LANES])
        k_ref[hd, :, 0:LANES] = kv[:, 0:LANES].astype(BF16)
        k_ref[hd, :, LANES:] = kr
        v_ref[hd] = kv[:, LANES:].astype(BF16)


def _inproj_o(x, g, wc, wl, qg, wuq, kvg, wukv, cos_t, sin_t, qscale):
    tm = TM_INO
    row = lambda i: (i, 0)
    head = lambda i: (0, i, 0)
    return pl.pallas_call(
        functools.partial(_inproj_o_kernel, qscale=qscale),
        grid=(SEQ // tm,),
        in_specs=[pl.BlockSpec((tm, D_MODEL), row),
                  _resident((1, D_MODEL)),
                  _resident(wc.shape), _resident(wl.shape),
                  _resident((1, Q_LORA)), _resident(wuq.shape),
                  _resident((1, KV_LORA)), _resident(wukv.shape),
                  pl.BlockSpec((tm, LANES), row), pl.BlockSpec((tm, LANES), row)],
        out_specs=[pl.BlockSpec((tm, D_C), row), pl.BlockSpec((tm, D_C), row),
                   pl.BlockSpec((MLA_HEADS, tm, QK_PAD), head),
                   pl.BlockSpec((MLA_HEADS, tm, QK_PAD), head),
                   pl.BlockSpec((MLA_HEADS, tm, V_HEAD), head)],
        out_shape=[jax.ShapeDtypeStruct((SEQ, D_C), F32), jax.ShapeDtypeStruct((SEQ, D_C), F32),
                   jax.ShapeDtypeStruct((MLA_HEADS, SEQ, QK_PAD), BF16),
                   jax.ShapeDtypeStruct((MLA_HEADS, SEQ, QK_PAD), BF16),
                   jax.ShapeDtypeStruct((MLA_HEADS, SEQ, V_HEAD), BF16)],
        compiler_params=_params(("arbitrary",)),
        name="inproj_o",
    )(x, g, wc, wl, qg, wuq, kvg, wukv, cos_t, sin_t)


TS_SC = 256
HALO_SC = 8


def _shortconv_kernel(prev_ref, main_ref, next_ref, bg_ref, w_ref, o_ref, ext_ref):
    i = pl.program_id(0)
    n = pl.num_programs(0)
    ext_ref[0:HALO_SC, :] = jnp.where(i > 0, prev_ref[...], 0.0)
    ext_ref[HALO_SC:HALO_SC + TS_SC, :] = main_ref[...]
    ext_ref[HALO_SC + TS_SC:, :] = jnp.where(i < n - 1, next_ref[...], 0.0)
    for c in range(0, D_C, 256):
        acc = w_ref[0:1, c:c + 256] * ext_ref[HALO_SC - 1:HALO_SC - 1 + TS_SC, c:c + 256]
        acc = acc + w_ref[1:2, c:c + 256] * ext_ref[HALO_SC:HALO_SC + TS_SC, c:c + 256]
        acc = acc + w_ref[2:3, c:c + 256] * ext_ref[HALO_SC + 1:HALO_SC + 1 + TS_SC, c:c + 256]
        o_ref[:, c:c + 256] = (bg_ref[:, c:c + 256] * acc).astype(BF16)


def _shortconv(bg, ch, w):
    r = TS_SC // HALO_SC
    nh = SEQ // HALO_SC
    return pl.pallas_call(
        _shortconv_kernel,
        grid=(SEQ // TS_SC,),
        in_specs=[pl.BlockSpec((HALO_SC, D_C), lambda i: (jnp.maximum(i * r - 1, 0), 0)),
                  pl.BlockSpec((TS_SC, D_C), lambda i: (i, 0)),
                  pl.BlockSpec((HALO_SC, D_C), lambda i: (jnp.minimum((i + 1) * r, nh - 1), 0)),
                  pl.BlockSpec((TS_SC, D_C), lambda i: (i, 0)),
                  _resident((3, D_C))],
        out_specs=pl.BlockSpec((TS_SC, D_C), lambda i: (i, 0)),
        out_shape=jax.ShapeDtypeStruct((SEQ, D_C), BF16),
        scratch_shapes=[pltpu.VMEM((TS_SC + 2 * HALO_SC, D_C), F32)],
        compiler_params=_params(("arbitrary",)),
        name="shortconv",
    )(ch, ch, ch, bg, w)


TQ_ATT = 256
TK_ATT = 512


def _attn_kernel(q_ref, k_ref, v_ref, o_ref, m_ref, l_ref, acc_ref):
    q = q_ref[0]
    m_ref[...] = jnp.full(m_ref.shape, -jnp.inf, F32)
    l_ref[...] = jnp.zeros(l_ref.shape, F32)
    acc_ref[...] = jnp.zeros(acc_ref.shape, F32)

    def body(kb, carry):
        off = pl.multiple_of(kb * TK_ATT, TK_ATT)
        k = k_ref[0, pl.ds(off, TK_ATT), :]
        v = v_ref[0, pl.ds(off, TK_ATT), :]
        s = lax.dot_general(q, k, (((1,), (1,)), ((), ())), preferred_element_type=F32)
        m_prev = m_ref[...]
        m_new = jnp.maximum(m_prev, jnp.max(s, axis=-1, keepdims=True))
        alpha = jnp.exp2(m_prev - m_new)
        p = jnp.exp2(s - m_new)
        l_ref[...] = alpha * l_ref[...] + jnp.sum(p, axis=-1, keepdims=True)
        acc_ref[...] = alpha * acc_ref[...] + _dot(p.astype(BF16), v)
        m_ref[...] = m_new
        return carry

    lax.fori_loop(0, SEQ // TK_ATT, body, 0)
    o_ref[...] = (acc_ref[...] / l_ref[...]).astype(BF16)


def _attention(q, k, v):
    return pl.pallas_call(
        _attn_kernel,
        grid=(MLA_HEADS, SEQ // TQ_ATT),
        in_specs=[pl.BlockSpec((1, TQ_ATT, QK_PAD), lambda h, i: (h, i, 0)),
                  pl.BlockSpec((1, SEQ, QK_PAD), lambda h, i: (h, 0, 0)),
                  pl.BlockSpec((1, SEQ, V_HEAD), lambda h, i: (h, 0, 0))],
        out_specs=pl.BlockSpec((TQ_ATT, V_HEAD), lambda h, i: (i, h)),
        out_shape=jax.ShapeDtypeStruct((SEQ, D_ATT), BF16),
        scratch_shapes=[pltpu.VMEM((TQ_ATT, 1), F32), pltpu.VMEM((TQ_ATT, 1), F32),
                        pltpu.VMEM((TQ_ATT, V_HEAD), F32)],
        compiler_params=_params(("arbitrary", "arbitrary")),
        name="mla_attention",
    )(q, k, v)


def _rope_tiles(w_rope):
    half = QK_ROPE // 2
    z = jnp.zeros((w_rope.shape[0], LANES - QK_ROPE), w_rope.dtype)
    a = jnp.concatenate([w_rope, z], axis=1)
    b = jnp.concatenate([w_rope[:, half:], w_rope[:, :half], z], axis=1)
    return a, b


def _rope_tables():
    inv = 1.0 / (ROPE_THETA ** (jnp.arange(0, QK_ROPE, 2, dtype=F32) / QK_ROPE))
    ang = jnp.arange(SEQ, dtype=F32)[:, None] * inv[None, :]
    cos, sin = jnp.cos(ang), jnp.sin(ang)
    z = jnp.zeros((SEQ, LANES - QK_ROPE), F32)
    return (jnp.concatenate([cos, cos, z], axis=1), jnp.concatenate([-sin, sin, z], axis=1))


def kernel(x, mix_norm_e, w_in_e, conv_a_w, conv_a_b, ln_a_g, ln_a_b, w_out_e, mix_norm_o, w_in_o,
           conv_c_w, q_norm_g, w_uq, kv_norm_g, w_ukv, w_out_o, mlp_norm, w_up, w_down, final_norm):
    xs = x[0]
    row = lambda v: v.reshape(1, -1)

    hglu, uf = _inproj_e(xs, row(mix_norm_e[0]), w_in_e[0].astype(BF16))
    ya = _conformer(hglu, conv_a_w[0], row(conv_a_b[0]), row(ln_a_g[0]), row(ln_a_b[0]))
    yb = _fourier_mix(uf)
    xs = _outproj_mlp(xs, ya, yb, w_out_e[0].astype(BF16), row(mlp_norm[0]),
                      w_up[0].astype(BF16), w_down[0].astype(BF16), row(final_norm), final=False)

    wi = w_in_o[0]
    o = 3 * D_C + Q_LORA + KV_LORA
    kra, krb = _rope_tiles(wi[:, o:])
    wl = jnp.concatenate([wi[:, 3 * D_C:o], kra, krb], axis=1).astype(BF16)
    wq = w_uq[0].reshape(Q_LORA, MLA_HEADS, QK_HEAD)
    qa, qb = _rope_tiles(wq[:, :, QK_NOPE:].reshape(Q_LORA * MLA_HEADS, QK_ROPE))
    wuq = jnp.concatenate([wq[:, :, :QK_NOPE], qa.reshape(Q_LORA, MLA_HEADS, LANES),
                           qb.reshape(Q_LORA, MLA_HEADS, LANES)], axis=2)
    wuq = wuq.reshape(Q_LORA, MLA_HEADS * Q_COLS).astype(BF16)
    cos_t, sin_t = _rope_tables()
    qscale = (QK_HEAD ** -0.5) * math.log2(math.e)
    bg, ch, q, k, v = _inproj_o(xs, row(mix_norm_o[0]), wi[:, :3 * D_C].astype(BF16), wl,
                                row(q_norm_g[0]), wuq, row(kv_norm_g[0]), w_ukv[0].astype(BF16),
                                cos_t, sin_t, qscale)
    yc = _shortconv(bg, ch, conv_c_w[0])
    yd = _attention(q, k, v)
    xs = _outproj_mlp(xs, yc, yd, w_out_o[0].astype(BF16), row(mlp_norm[1]),
                      w_up[1].astype(BF16), w_down[1].astype(BF16), row(final_norm), final=True)
    return xs[None]
```

```python
import functools
import math

import numpy as np
import jax
import jax.numpy as jnp
from jax import lax
from jax.experimental import pallas as pl
from jax.experimental.pallas import tpu as pltpu

D_MODEL = 2048
SEQ = 16384
D_A = 1024
CONF_KERNEL = 31
D_B = 1024
FNET_GROUPS = 8
FNET_GROUP_CH = 128
D_C = 1024
MLA_HEADS = 8
Q_LORA = 512
KV_LORA = 256
QK_NOPE = 128
QK_ROPE = 64
V_HEAD = 128
QK_HEAD = QK_NOPE + QK_ROPE
D_ATT = MLA_HEADS * V_HEAD
ROPE_THETA = 10000.0
D_FF = 4 * D_MODEL
EPS = 1e-6

LANES = 128
QK_PAD = 2 * LANES
V_ROWS = V_HEAD + 16
FFT_N1 = 128
FFT_N2 = 128
VMEM_LIMIT = 56 * 1024 * 1024

BF16 = jnp.bfloat16
F32 = jnp.float32


def _dot(a, b):
    return jnp.dot(a, b, preferred_element_type=F32)


def _rms(x, g):
    return x * lax.rsqrt(jnp.mean(x * x, axis=-1, keepdims=True) + EPS) * g


def _params(sem):
    return pltpu.CompilerParams(dimension_semantics=sem, vmem_limit_bytes=VMEM_LIMIT)


def _resident(shape):
    nd = len(shape)
    return pl.BlockSpec(shape, lambda *_: (0,) * nd, pipeline_mode=pl.Buffered(1))


TM_IN = 512


def _inproj_e_kernel(x_ref, g_ref, w_ref, glu_ref, uf_ref):
    h = _rms(x_ref[...], g_ref[...]).astype(BF16)
    for c in range(0, D_A, 512):
        val = _dot(h, w_ref[:, c:c + 512])
        gate = _dot(h, w_ref[:, D_A + c:D_A + c + 512])
        glu_ref[:, c:c + 512] = val * jax.nn.sigmoid(gate)
    for c in range(0, D_B, 512):
        uf_ref[:, c:c + 512] = _dot(h, w_ref[:, 2 * D_A + c:2 * D_A + c + 512]).astype(BF16)


def _inproj_e(x, g, w):
    return pl.pallas_call(
        _inproj_e_kernel,
        grid=(SEQ // TM_IN,),
        in_specs=[pl.BlockSpec((TM_IN, D_MODEL), lambda i: (i, 0)),
                  _resident((1, D_MODEL)),
                  _resident((D_MODEL, 2 * D_A + D_B))],
        out_specs=[pl.BlockSpec((TM_IN, D_A), lambda i: (i, 0)),
                   pl.BlockSpec((TM_IN, D_B), lambda i: (i, 0))],
        out_shape=[jax.ShapeDtypeStruct((SEQ, D_A), F32),
                   jax.ShapeDtypeStruct((SEQ, D_B), BF16)],
        compiler_params=_params(("arbitrary",)),
        name="inproj_e",
    )(x, g, w)


TC_CONF = 128
HALO_CONF = 16


def _conformer_kernel(prev_ref, main_ref, next_ref, w_ref, b_ref, lg_ref, lb_ref, o_ref,
                      hext_ref, conv_ref):
    i = pl.program_id(0)
    n = pl.num_programs(0)
    hext_ref[0:HALO_CONF, :] = jnp.where(i > 0, prev_ref[...], 0.0)
    hext_ref[HALO_CONF:HALO_CONF + TC_CONF, :] = main_ref[...]
    hext_ref[HALO_CONF + TC_CONF:, :] = jnp.where(i < n - 1, next_ref[...], 0.0)
    base = HALO_CONF - CONF_KERNEL // 2
    for c in range(0, D_A, LANES):
        acc = jnp.zeros((TC_CONF, LANES), F32) + b_ref[:, c:c + LANES]
        for t in range(CONF_KERNEL):
            acc = acc + w_ref[t:t + 1, c:c + LANES] * hext_ref[base + t:base + t + TC_CONF, c:c + LANES]
        conv_ref[:, c:c + LANES] = acc
    y = conv_ref[...]
    mu = jnp.mean(y, axis=-1, keepdims=True)
    d = y - mu
    var = jnp.mean(d * d, axis=-1, keepdims=True)
    z = d * lax.rsqrt(var + EPS) * lg_ref[...] + lb_ref[...]
    o_ref[...] = (z * jax.nn.sigmoid(z)).astype(BF16)


def _conformer(hglu, w, b, lg, lb):
    r = TC_CONF // HALO_CONF
    nh = SEQ // HALO_CONF
    return pl.pallas_call(
        _conformer_kernel,
        grid=(SEQ // TC_CONF,),
        in_specs=[pl.BlockSpec((HALO_CONF, D_A), lambda i: (jnp.maximum(i * r - 1, 0), 0)),
                  pl.BlockSpec((TC_CONF, D_A), lambda i: (i, 0)),
                  pl.BlockSpec((HALO_CONF, D_A), lambda i: (jnp.minimum((i + 1) * r, nh - 1), 0)),
                  _resident((CONF_KERNEL, D_A)),
                  _resident((1, D_A)), _resident((1, D_A)), _resident((1, D_A))],
        out_specs=pl.BlockSpec((TC_CONF, D_A), lambda i: (i, 0)),
        out_shape=jax.ShapeDtypeStruct((SEQ, D_A), BF16),
        scratch_shapes=[pltpu.VMEM((TC_CONF + 2 * HALO_CONF, D_A), F32),
                        pltpu.VMEM((TC_CONF, D_A), F32)],
        compiler_params=_params(("arbitrary",)),
        name="conformer",
    )(hglu, hglu, hglu, w, b, lg, lb)


TN_FFT1 = 8192
TK1_FFT2 = 8


def _fft1_kernel(x_ref, f_ref, tr_ref, ti_ref):
    for c in range(0, TN_FFT1, 1024):
        t = _dot(f_ref[...], x_ref[:, c:c + 1024])
        tr_ref[:, c:c + 1024] = t[:FFT_N1].astype(BF16)
        ti_ref[:, c:c + 1024] = t[FFT_N1:].astype(BF16)


def _fft1(x2d, f_stack):
    ncol = FFT_N2 * D_B
    return pl.pallas_call(
        _fft1_kernel,
        grid=(ncol // TN_FFT1,),
        in_specs=[pl.BlockSpec((FFT_N1, TN_FFT1), lambda j: (0, j)),
                  _resident((2 * FFT_N1, FFT_N1))],
        out_specs=[pl.BlockSpec((FFT_N1, TN_FFT1), lambda j: (0, j)),
                   pl.BlockSpec((FFT_N1, TN_FFT1), lambda j: (0, j))],
        out_shape=[jax.ShapeDtypeStruct((FFT_N1, ncol), BF16),
                   jax.ShapeDtypeStruct((FFT_N1, ncol), BF16)],
        compiler_params=_params(("arbitrary",)),
        name="fft_stage1",
    )(x2d, f_stack)


def _fft2_kernel(tr_ref, ti_ref, m_ref, cs_ref, o_ref):
    for j in range(TK1_FFT2):
        t = jnp.concatenate([tr_ref[j], ti_ref[j]], axis=0)
        z = _dot(m_ref[j], t)
        zr = z[:FFT_N2].astype(BF16)
        zi = z[FFT_N2:].astype(BF16)
        for g in range(FNET_GROUPS):
            lo = g * FNET_GROUP_CH
            zz = jnp.concatenate([zr[:, lo:lo + FNET_GROUP_CH], zi[:, lo:lo + FNET_GROUP_CH]], axis=1)
            o_ref[:, j * D_B + lo:j * D_B + lo + FNET_GROUP_CH] = _dot(zz, cs_ref[...]).astype(BF16)


def _fft2(tr3, ti3, m_full, cs):
    return pl.pallas_call(
        _fft2_kernel,
        grid=(FFT_N1 // TK1_FFT2,),
        in_specs=[pl.BlockSpec((TK1_FFT2, FFT_N2, D_B), lambda i: (i, 0, 0)),
                  pl.BlockSpec((TK1_FFT2, FFT_N2, D_B), lambda i: (i, 0, 0)),
                  pl.BlockSpec((TK1_FFT2, 2 * FFT_N2, 2 * FFT_N2), lambda i: (i, 0, 0)),
                  _resident((2 * FNET_GROUP_CH, FNET_GROUP_CH))],
        out_specs=pl.BlockSpec((FFT_N2, TK1_FFT2 * D_B), lambda i: (0, i)),
        out_shape=jax.ShapeDtypeStruct((FFT_N2, FFT_N1 * D_B), BF16),
        compiler_params=_params(("arbitrary",)),
        name="fft_stage2",
    )(tr3, ti3, m_full, cs)


def _dft_tables():
    n = np.arange(128)
    ph = 2.0 * np.pi * ((n[:, None] * n[None, :]) % 128) / 128.0
    fr, fi = np.cos(ph), -np.sin(ph)
    f_stack = np.concatenate([fr, fi], axis=0) / math.sqrt(SEQ)
    tw_ph = 2.0 * np.pi * (n[:, None] * n[None, :]) / SEQ
    twr, twi = np.cos(tw_ph), -np.sin(tw_ph)
    cs = np.concatenate([np.cos(ph), np.sin(ph)], axis=0) / math.sqrt(FNET_GROUP_CH)
    return (jnp.asarray(f_stack, F32).astype(BF16), jnp.asarray(fr, F32), jnp.asarray(fi, F32),
            jnp.asarray(twr, F32), jnp.asarray(twi, F32), jnp.asarray(cs, F32).astype(BF16))


def _fourier_mix(uf):
    f_stack, fr, fi, twr, twi, cs = _dft_tables()
    mr = fr[None] * twr[:, None, :] - fi[None] * twi[:, None, :]
    mi = fr[None] * twi[:, None, :] + fi[None] * twr[:, None, :]
    m_full = jnp.concatenate([jnp.concatenate([mr, -mi], axis=2),
                              jnp.concatenate([mi, mr], axis=2)], axis=1).astype(BF16)
    tr, ti = _fft1(uf.reshape(FFT_N1, FFT_N2 * D_B), f_stack)
    y2d = _fft2(tr.reshape(FFT_N1, FFT_N2, D_B), ti.reshape(FFT_N1, FFT_N2, D_B), m_full, cs)
    return y2d.reshape(SEQ, D_B)


TM_MLP = 512
TF_MLP = 1024


def _mlp_kernel(x_ref, ya_ref, yb_ref, wo_ref, g_ref, wup_ref, wdn_ref, gf_ref, o_ref, h_ref, *,
                final):
    f = pl.program_id(1)
    half = wo_ref.shape[0] // 2

    @pl.when(f == 0)
    def _():
        for c in range(0, D_MODEL, 512):
            o_ref[:, c:c + 512] = (x_ref[:, c:c + 512]
                                   + _dot(ya_ref[...], wo_ref[0:half, c:c + 512])
                                   + _dot(yb_ref[...], wo_ref[half:, c:c + 512]))
        h_ref[...] = _rms(o_ref[...], g_ref[...]).astype(BF16)

    a = _dot(h_ref[...], wup_ref[...])
    a = jnp.square(jnp.maximum(a, 0.0)).astype(BF16)
    for c in range(0, D_MODEL, 512):
        o_ref[:, c:c + 512] += _dot(a, wdn_ref[:, c:c + 512])

    if final:
        @pl.when(f == pl.num_programs(1) - 1)
        def _():
            o_ref[...] = _rms(o_ref[...], gf_ref[...])


def _outproj_mlp(x, ya, yb, wo, g, wup, wdn, gf, final):
    return pl.pallas_call(
        functools.partial(_mlp_kernel, final=final),
        grid=(SEQ // TM_MLP, D_FF // TF_MLP),
        in_specs=[pl.BlockSpec((TM_MLP, D_MODEL), lambda i, f: (i, 0)),
                  pl.BlockSpec((TM_MLP, ya.shape[1]), lambda i, f: (i, 0)),
                  pl.BlockSpec((TM_MLP, yb.shape[1]), lambda i, f: (i, 0)),
                  _resident(wo.shape),
                  _resident((1, D_MODEL)),
                  pl.BlockSpec((D_MODEL, TF_MLP), lambda i, f: (0, f)),
                  pl.BlockSpec((TF_MLP, D_MODEL), lambda i, f: (f, 0)),
                  _resident((1, D_MODEL))],
        out_specs=pl.BlockSpec((TM_MLP, D_MODEL), lambda i, f: (i, 0)),
        out_shape=jax.ShapeDtypeStruct((SEQ, D_MODEL), F32),
        scratch_shapes=[pltpu.VMEM((TM_MLP, D_MODEL), BF16)],
        compiler_params=_params(("arbitrary", "arbitrary")),
        name="outproj_mlp_final" if final else "outproj_mlp",
    )(x, ya, yb, wo, g, wup, wdn, gf)


TM_INO = 256
Q_COLS = 3 * LANES


def _inproj_o_kernel(x_ref, g_ref, wc_ref, wl_ref, qg_ref, wuq_ref, kvg_ref, wukv_ref, cos_ref, sin_ref,
                     bg_ref, ch_ref, q_ref, k_ref, v_ref, *, qscale):
    h = _rms(x_ref[...], g_ref[...]).astype(BF16)
    for c in range(0, D_C, 512):
        bg_ref[:, c:c + 512] = _dot(h, wc_ref[:, c:c + 512])
        ch_ref[:, c:c + 512] = (_dot(h, wc_ref[:, D_C + c:D_C + c + 512])
                                * _dot(h, wc_ref[:, 2 * D_C + c:2 * D_C + c + 512]))
    cosv = cos_ref[...]
    sinv = sin_ref[...]
    tail = lax.broadcasted_iota(jnp.int32, (V_ROWS - V_HEAD, x_ref.shape[0]), 0)
    ones_row = jnp.where(tail == 0, 1.0, 0.0).astype(BF16)
    lat = _dot(h, wl_ref[...])
    o = Q_LORA + KV_LORA
    kr = (lat[:, o:o + LANES] * cosv + lat[:, o + LANES:o + 2 * LANES] * sinv).astype(BF16)
    cq = _rms(lat[:, :Q_LORA], qg_ref[...]).astype(BF16)
    ckv = _rms(lat[:, Q_LORA:o], kvg_ref[...]).astype(BF16)
    for hd in range(MLA_HEADS):
        qh = _dot(cq, wuq_ref[:, hd * Q_COLS:(hd + 1) * Q_COLS])
        q_ref[hd, 0:LANES, :] = (qh[:, 0:LANES] * qscale).T.astype(BF16)
        q_ref[hd, LANES:, :] = ((qh[:, LANES:2 * LANES] * cosv + qh[:, 2 * LANES:] * sinv)
                                * qscale).T.astype(BF16)
        kv = _dot(ckv, wukv_ref[:, hd * 2 * LANES:(hd + 1) * 2 * LANES])
        k_ref[hd, :, 0:LANES] = kv[:, 0:LANES].astype(BF16)
        k_ref[hd, :, LANES:] = kr
        v_ref[hd, 0, 0:V_HEAD, :] = kv[:, LANES:].T.astype(BF16)
        v_ref[hd, 0, V_HEAD:, :] = ones_row


def _inproj_o(x, g, wc, wl, qg, wuq, kvg, wukv, cos_t, sin_t, qscale):
    tm = TM_INO
    row = lambda i: (i, 0)
    head = lambda i: (0, i, 0)
    return pl.pallas_call(
        functools.partial(_inproj_o_kernel, qscale=qscale),
        grid=(SEQ // tm,),
        in_specs=[pl.BlockSpec((tm, D_MODEL), row),
                  _resident((1, D_MODEL)),
                  _resident(wc.shape), _resident(wl.shape),
                  _resident((1, Q_LORA)), _resident(wuq.shape),
                  _resident((1, KV_LORA)), _resident(wukv.shape),
                  pl.BlockSpec((tm, LANES), row), pl.BlockSpec((tm, LANES), row)],
        out_specs=[pl.BlockSpec((tm, D_C), row), pl.BlockSpec((tm, D_C), row),
                   pl.BlockSpec((MLA_HEADS, QK_PAD, tm), lambda i: (0, 0, i)),
                   pl.BlockSpec((MLA_HEADS, tm, QK_PAD), head),
                   pl.BlockSpec((MLA_HEADS, 1, V_ROWS, tm),
                                lambda i: (0, i // (TK_ATT // tm), 0, i % (TK_ATT // tm)))],
        out_shape=[jax.ShapeDtypeStruct((SEQ, D_C), F32), jax.ShapeDtypeStruct((SEQ, D_C), F32),
                   jax.ShapeDtypeStruct((MLA_HEADS, QK_PAD, SEQ), BF16),
                   jax.ShapeDtypeStruct((MLA_HEADS, SEQ, QK_PAD), BF16),
                   jax.ShapeDtypeStruct((MLA_HEADS, SEQ // TK_ATT, V_ROWS, TK_ATT), BF16)],
        compiler_params=_params(("arbitrary",)),
        name="inproj_o",
    )(x, g, wc, wl, qg, wuq, kvg, wukv, cos_t, sin_t)


TS_SC = 256
HALO_SC = 8


def _shortconv_kernel(prev_ref, main_ref, next_ref, bg_ref, w_ref, o_ref, ext_ref):
    i = pl.program_id(0)
    n = pl.num_programs(0)
    ext_ref[0:HALO_SC, :] = jnp.where(i > 0, prev_ref[...], 0.0)
    ext_ref[HALO_SC:HALO_SC + TS_SC, :] = main_ref[...]
    ext_ref[HALO_SC + TS_SC:, :] = jnp.where(i < n - 1, next_ref[...], 0.0)
    for c in range(0, D_C, 256):
        acc = w_ref[0:1, c:c + 256] * ext_ref[HALO_SC - 1:HALO_SC - 1 + TS_SC, c:c + 256]
        acc = acc + w_ref[1:2, c:c + 256] * ext_ref[HALO_SC:HALO_SC + TS_SC, c:c + 256]
        acc = acc + w_ref[2:3, c:c + 256] * ext_ref[HALO_SC + 1:HALO_SC + 1 + TS_SC, c:c + 256]
        o_ref[:, c:c + 256] = (bg_ref[:, c:c + 256] * acc).astype(BF16)


def _shortconv(bg, ch, w):
    r = TS_SC // HALO_SC
    nh = SEQ // HALO_SC
    return pl.pallas_call(
        _shortconv_kernel,
        grid=(SEQ // TS_SC,),
        in_specs=[pl.BlockSpec((HALO_SC, D_C), lambda i: (jnp.maximum(i * r - 1, 0), 0)),
                  pl.BlockSpec((TS_SC, D_C), lambda i: (i, 0)),
                  pl.BlockSpec((HALO_SC, D_C), lambda i: (jnp.minimum((i + 1) * r, nh - 1), 0)),
                  pl.BlockSpec((TS_SC, D_C), lambda i: (i, 0)),
                  _resident((3, D_C))],
        out_specs=pl.BlockSpec((TS_SC, D_C), lambda i: (i, 0)),
        out_shape=jax.ShapeDtypeStruct((SEQ, D_C), BF16),
        scratch_shapes=[pltpu.VMEM((TS_SC + 2 * HALO_SC, D_C), F32)],
        compiler_params=_params(("arbitrary",)),
        name="shortconv",
    )(ch, ch, ch, bg, w)


TQ_ATT = 1024
TK_ATT = 512
NKB_ATT = SEQ // TK_ATT
LW_ATT = 256


def _attn_kernel(q_ref, k_ref, v_ref, o_ref, s0, s1, p0, p1, a0, a1, mb0, mb1, m_ref, acc_ref):
    def scores(kb, s_ref, mb_ref):
        off = pl.multiple_of(kb * TK_ATT, TK_ATT)
        k = k_ref[0, pl.ds(off, TK_ATT), :]
        for c in range(0, TQ_ATT, LW_ATT):
            st = _dot(k, q_ref[0, :, c:c + LW_ATT])
            s_ref[:, c:c + LW_ATT] = st
            mb_ref[:, c:c + LW_ATT] = jnp.max(st, axis=0, keepdims=True)

    def softmax(s_ref, mb_ref, p_ref, a_ref):
        for c in range(0, TQ_ATT, LW_ATT):
            sl = slice(c, c + LW_ATT)
            m_prev = m_ref[:, sl]
            m_new = jnp.maximum(m_prev, mb_ref[:, sl])
            a_ref[:, sl] = jnp.exp2(m_prev - m_new)
            m_ref[:, sl] = m_new
            p_ref[:, sl] = jnp.exp2(s_ref[:, sl] - m_new).astype(BF16)

    def values(kb, p_ref, a_ref):
        vt = v_ref[0, kb]
        for c in range(0, TQ_ATT, LW_ATT):
            sl = slice(c, c + LW_ATT)
            acc_ref[:, sl] = a_ref[:, sl] * acc_ref[:, sl] + _dot(vt, p_ref[:, sl])

    m_ref[...] = jnp.full(m_ref.shape, -jnp.inf, F32)
    acc_ref[...] = jnp.zeros(acc_ref.shape, F32)
    p1[...] = jnp.zeros(p1.shape, BF16)
    a1[...] = jnp.ones(a1.shape, F32)
    scores(0, s0, mb0)

    def body(i, carry):
        j = 2 * i
        scores(j + 1, s1, mb1)
        softmax(s0, mb0, p0, a0)
        values(jnp.maximum(j - 1, 0), p1, a1)
        scores(jnp.minimum(j + 2, NKB_ATT - 1), s0, mb0)
        softmax(s1, mb1, p1, a1)
        values(j, p0, a0)
        return carry

    lax.fori_loop(0, NKB_ATT // 2, body, 0)
    values(NKB_ATT - 1, p1, a1)
    o_ref[...] = (acc_ref[0:V_HEAD, :] / acc_ref[V_HEAD:V_HEAD + 1, :]).T.astype(BF16)


def _attention(qt, k, vt):
    return pl.pallas_call(
        _attn_kernel,
        grid=(MLA_HEADS, SEQ // TQ_ATT),
        in_specs=[pl.BlockSpec((1, QK_PAD, TQ_ATT), lambda h, i: (h, 0, i)),
                  pl.BlockSpec((1, SEQ, QK_PAD), lambda h, i: (h, 0, 0)),
                  pl.BlockSpec((1, SEQ // TK_ATT, V_ROWS, TK_ATT), lambda h, i: (h, 0, 0, 0))],
        out_specs=pl.BlockSpec((TQ_ATT, V_HEAD), lambda h, i: (i, h)),
        out_shape=jax.ShapeDtypeStruct((SEQ, D_ATT), BF16),
        scratch_shapes=[pltpu.VMEM((TK_ATT, TQ_ATT), F32), pltpu.VMEM((TK_ATT, TQ_ATT), F32),
                        pltpu.VMEM((TK_ATT, TQ_ATT), BF16), pltpu.VMEM((TK_ATT, TQ_ATT), BF16),
                        pltpu.VMEM((1, TQ_ATT), F32), pltpu.VMEM((1, TQ_ATT), F32),
                        pltpu.VMEM((1, TQ_ATT), F32), pltpu.VMEM((1, TQ_ATT), F32),
                        pltpu.VMEM((1, TQ_ATT), F32),
                        pltpu.VMEM((V_ROWS, TQ_ATT), F32)],
        compiler_params=_params(("arbitrary", "arbitrary")),
        name="mla_attention",
    )(qt, k, vt)


def _rope_tiles(w_rope):
    half = QK_ROPE // 2
    z = jnp.zeros((w_rope.shape[0], LANES - QK_ROPE), w_rope.dtype)
    a = jnp.concatenate([w_rope, z], axis=1)
    b = jnp.concatenate([w_rope[:, half:], w_rope[:, :half], z], axis=1)
    return a, b


def _rope_tables():
    inv = 1.0 / (ROPE_THETA ** (jnp.arange(0, QK_ROPE, 2, dtype=F32) / QK_ROPE))
    ang = jnp.arange(SEQ, dtype=F32)[:, None] * inv[None, :]
    cos, sin = jnp.cos(ang), jnp.sin(ang)
    z = jnp.zeros((SEQ, LANES - QK_ROPE), F32)
    return (jnp.concatenate([cos, cos, z], axis=1), jnp.concatenate([-sin, sin, z], axis=1))


def kernel(x, mix_norm_e, w_in_e, conv_a_w, conv_a_b, ln_a_g, ln_a_b, w_out_e, mix_norm_o, w_in_o,
           conv_c_w, q_norm_g, w_uq, kv_norm_g, w_ukv, w_out_o, mlp_norm, w_up, w_down, final_norm):
    xs = x[0]
    row = lambda v: v.reshape(1, -1)

    hglu, uf = _inproj_e(xs, row(mix_norm_e[0]), w_in_e[0].astype(BF16))
    ya = _conformer(hglu, conv_a_w[0], row(conv_a_b[0]), row(ln_a_g[0]), row(ln_a_b[0]))
    yb = _fourier_mix(uf)
    xs = _outproj_mlp(xs, ya, yb, w_out_e[0].astype(BF16), row(mlp_norm[0]),
                      w_up[0].astype(BF16), w_down[0].astype(BF16), row(final_norm), final=False)

    wi = w_in_o[0]
    o = 3 * D_C + Q_LORA + KV_LORA
    kra, krb = _rope_tiles(wi[:, o:])
    wl = jnp.concatenate([wi[:, 3 * D_C:o], kra, krb], axis=1).astype(BF16)
    wq = w_uq[0].reshape(Q_LORA, MLA_HEADS, QK_HEAD)
    qa, qb = _rope_tiles(wq[:, :, QK_NOPE:].reshape(Q_LORA * MLA_HEADS, QK_ROPE))
    wuq = jnp.concatenate([wq[:, :, :QK_NOPE], qa.reshape(Q_LORA, MLA_HEADS, LANES),
                           qb.reshape(Q_LORA, MLA_HEADS, LANES)], axis=2)
    wuq = wuq.reshape(Q_LORA, MLA_HEADS * Q_COLS).astype(BF16)
    cos_t, sin_t = _rope_tables()
    qscale = (QK_HEAD ** -0.5) * math.log2(math.e)
    bg, ch, q, k, v = _inproj_o(xs, row(mix_norm_o[0]), wi[:, :3 * D_C].astype(BF16), wl,
                                row(q_norm_g[0]), wuq, row(kv_norm_g[0]), w_ukv[0].astype(BF16),
                                cos_t, sin_t, qscale)
    yc = _shortconv(bg, ch, conv_c_w[0])
    yd = _attention(q, k, v)
    xs = _outproj_mlp(xs, yc, yd, w_out_o[0].astype(BF16), row(mlp_norm[1]),
                      w_up[1].astype(BF16), w_down[1].astype(BF16), row(final_norm), final=True)
    return xs[None]
```

```python
import functools
import math

import numpy as np
import jax
import jax.numpy as jnp
from jax import lax
from jax.experimental import pallas as pl
from jax.experimental.pallas import tpu as pltpu

D_MODEL = 2048
SEQ = 16384
D_A = 1024
CONF_KERNEL = 31
D_B = 1024
FNET_GROUPS = 8
FNET_GROUP_CH = 128
D_C = 1024
MLA_HEADS = 8
Q_LORA = 512
KV_LORA = 256
QK_NOPE = 128
QK_ROPE = 64
V_HEAD = 128
QK_HEAD = QK_NOPE + QK_ROPE
D_ATT = MLA_HEADS * V_HEAD
ROPE_THETA = 10000.0
D_FF = 4 * D_MODEL
EPS = 1e-6

LANES = 128
QK_PAD = 2 * LANES
V_ROWS = V_HEAD + 16
FFT_N1 = 128
FFT_N2 = 128
VMEM_LIMIT = 56 * 1024 * 1024

BF16 = jnp.bfloat16
F32 = jnp.float32


def _dot(a, b):
    return jnp.dot(a, b, preferred_element_type=F32)


def _rms(x, g):
    return x * lax.rsqrt(jnp.mean(x * x, axis=-1, keepdims=True) + EPS) * g


def _params(sem):
    return pltpu.CompilerParams(dimension_semantics=sem, vmem_limit_bytes=VMEM_LIMIT)


def _resident(shape):
    nd = len(shape)
    return pl.BlockSpec(shape, lambda *_: (0,) * nd, pipeline_mode=pl.Buffered(1))


TM_IN = 512


def _inproj_e_kernel(x_ref, g_ref, w_ref, glu_ref, uf_ref):
    h = _rms(x_ref[...], g_ref[...]).astype(BF16)
    for c in range(0, D_A, 512):
        val = _dot(h, w_ref[:, c:c + 512])
        gate = _dot(h, w_ref[:, D_A + c:D_A + c + 512])
        glu_ref[:, c:c + 512] = val * jax.nn.sigmoid(gate)
    for c in range(0, D_B, 512):
        uf_ref[:, c:c + 512] = _dot(h, w_ref[:, 2 * D_A + c:2 * D_A + c + 512]).astype(BF16)


def _inproj_e(x, g, w):
    return pl.pallas_call(
        _inproj_e_kernel,
        grid=(SEQ // TM_IN,),
        in_specs=[pl.BlockSpec((TM_IN, D_MODEL), lambda i: (i, 0)),
                  _resident((1, D_MODEL)),
                  _resident((D_MODEL, 2 * D_A + D_B))],
        out_specs=[pl.BlockSpec((TM_IN, D_A), lambda i: (i, 0)),
                   pl.BlockSpec((TM_IN, D_B), lambda i: (i, 0))],
        out_shape=[jax.ShapeDtypeStruct((SEQ, D_A), F32),
                   jax.ShapeDtypeStruct((SEQ, D_B), BF16)],
        compiler_params=_params(("arbitrary",)),
        name="inproj_e",
    )(x, g, w)


TC_CONF = 128
HALO_CONF = 16


SUBLANES = 8
CONF_Q = (-2, -1, 0, 1, 2)


def _conf_tap_plan():
    plan = np.full((len(CONF_Q), SUBLANES, SUBLANES), -1, np.int32)
    for qi, q in enumerate(CONF_Q):
        for r in range(SUBLANES):
            for i in range(SUBLANES):
                d = 8 * q + r if i + r < SUBLANES else 8 * (q - 1) + r
                if abs(d) <= CONF_KERNEL // 2:
                    plan[qi, r, i] = d + CONF_KERNEL // 2
    return plan


def _conformer_kernel(prev_ref, main_ref, next_ref, w_ref, b_ref, lg_ref, lb_ref, o_ref,
                      hext_ref, conv_ref, *, live):
    i = pl.program_id(0)
    n = pl.num_programs(0)
    hext_ref[0:HALO_CONF, :] = jnp.where(i > 0, prev_ref[...], 0.0)
    hext_ref[HALO_CONF:HALO_CONF + TC_CONF, :] = main_ref[...]
    hext_ref[HALO_CONF + TC_CONF:, :] = jnp.where(i < n - 1, next_ref[...], 0.0)
    nt_out = TC_CONF // SUBLANES
    nt_ext = (TC_CONF + 2 * HALO_CONF) // SUBLANES
    t0 = HALO_CONF // SUBLANES
    for c in range(0, D_A, LANES):
        x3 = hext_ref[:, c:c + LANES].reshape(nt_ext, SUBLANES, LANES)
        acc = jnp.zeros((nt_out, SUBLANES, LANES), F32) + b_ref[:, c:c + LANES]
        for r in range(SUBLANES):
            xr = x3 if r == 0 else pltpu.roll(x3, SUBLANES - r, axis=1)
            for qi, q in enumerate(CONF_Q):
                if live[qi][r]:
                    row = (qi * SUBLANES + r) * SUBLANES
                    acc = acc + w_ref[row:row + SUBLANES, c:c + LANES] * xr[t0 + q:t0 + q + nt_out]
        conv_ref[:, c:c + LANES] = acc.reshape(TC_CONF, LANES)
    y = conv_ref[...]
    mu = jnp.mean(y, axis=-1, keepdims=True)
    d = y - mu
    var = jnp.mean(d * d, axis=-1, keepdims=True)
    z = d * lax.rsqrt(var + EPS) * lg_ref[...] + lb_ref[...]
    o_ref[...] = (z * jax.nn.sigmoid(z)).astype(BF16)


def _conformer(hglu, w, b, lg, lb):
    r = TC_CONF // HALO_CONF
    nh = SEQ // HALO_CONF
    plan = _conf_tap_plan()
    live = tuple(tuple(bool((plan[qi, rr] >= 0).any()) for rr in range(SUBLANES))
                 for qi in range(len(CONF_Q)))
    wtab = jnp.where((plan >= 0)[..., None], w[np.maximum(plan, 0)], 0.0).reshape(-1, D_A)
    return pl.pallas_call(
        functools.partial(_conformer_kernel, live=live),
        grid=(SEQ // TC_CONF,),
        in_specs=[pl.BlockSpec((HALO_CONF, D_A), lambda i: (jnp.maximum(i * r - 1, 0), 0)),
                  pl.BlockSpec((TC_CONF, D_A), lambda i: (i, 0)),
                  pl.BlockSpec((HALO_CONF, D_A), lambda i: (jnp.minimum((i + 1) * r, nh - 1), 0)),
                  _resident((len(CONF_Q) * SUBLANES * SUBLANES, D_A)),
                  _resident((1, D_A)), _resident((1, D_A)), _resident((1, D_A))],
        out_specs=pl.BlockSpec((TC_CONF, D_A), lambda i: (i, 0)),
        out_shape=jax.ShapeDtypeStruct((SEQ, D_A), BF16),
        scratch_shapes=[pltpu.VMEM((TC_CONF + 2 * HALO_CONF, D_A), F32),
                        pltpu.VMEM((TC_CONF, D_A), F32)],
        compiler_params=_params(("arbitrary",)),
        name="conformer",
    )(hglu, hglu, hglu, wtab, b, lg, lb)


TN_FFT1 = 8192
TK1_FFT2 = 8


def _fft1_kernel(x_ref, f_ref, tr_ref, ti_ref):
    for c in range(0, TN_FFT1, 1024):
        t = _dot(f_ref[...], x_ref[:, c:c + 1024])
        tr_ref[:, c:c + 1024] = t[:FFT_N1].astype(BF16)
        ti_ref[:, c:c + 1024] = t[FFT_N1:].astype(BF16)


def _fft1(x2d, f_stack):
    ncol = FFT_N2 * D_B
    return pl.pallas_call(
        _fft1_kernel,
        grid=(ncol // TN_FFT1,),
        in_specs=[pl.BlockSpec((FFT_N1, TN_FFT1), lambda j: (0, j)),
                  _resident((2 * FFT_N1, FFT_N1))],
        out_specs=[pl.BlockSpec((FFT_N1, TN_FFT1), lambda j: (0, j)),
                   pl.BlockSpec((FFT_N1, TN_FFT1), lambda j: (0, j))],
        out_shape=[jax.ShapeDtypeStruct((FFT_N1, ncol), BF16),
                   jax.ShapeDtypeStruct((FFT_N1, ncol), BF16)],
        compiler_params=_params(("arbitrary",)),
        name="fft_stage1",
    )(x2d, f_stack)


def _fft2_kernel(tr_ref, ti_ref, m_ref, cs_ref, o_ref):
    for j in range(TK1_FFT2):
        t = jnp.concatenate([tr_ref[j], ti_ref[j]], axis=0)
        z = _dot(m_ref[j], t)
        zr = z[:FFT_N2].astype(BF16)
        zi = z[FFT_N2:].astype(BF16)
        for g in range(FNET_GROUPS):
            lo = g * FNET_GROUP_CH
            zz = jnp.concatenate([zr[:, lo:lo + FNET_GROUP_CH], zi[:, lo:lo + FNET_GROUP_CH]], axis=1)
            o_ref[:, j * D_B + lo:j * D_B + lo + FNET_GROUP_CH] = _dot(zz, cs_ref[...]).astype(BF16)


def _fft2(tr3, ti3, m_full, cs):
    return pl.pallas_call(
        _fft2_kernel,
        grid=(FFT_N1 // TK1_FFT2,),
        in_specs=[pl.BlockSpec((TK1_FFT2, FFT_N2, D_B), lambda i: (i, 0, 0)),
                  pl.BlockSpec((TK1_FFT2, FFT_N2, D_B), lambda i: (i, 0, 0)),
                  pl.BlockSpec((TK1_FFT2, 2 * FFT_N2, 2 * FFT_N2), lambda i: (i, 0, 0)),
                  _resident((2 * FNET_GROUP_CH, FNET_GROUP_CH))],
        out_specs=pl.BlockSpec((FFT_N2, TK1_FFT2 * D_B), lambda i: (0, i)),
        out_shape=jax.ShapeDtypeStruct((FFT_N2, FFT_N1 * D_B), BF16),
        compiler_params=_params(("arbitrary",)),
        name="fft_stage2",
    )(tr3, ti3, m_full, cs)


def _dft_tables():
    n = np.arange(128)
    ph = 2.0 * np.pi * ((n[:, None] * n[None, :]) % 128) / 128.0
    fr, fi = np.cos(ph), -np.sin(ph)
    f_stack = np.concatenate([fr, fi], axis=0) / math.sqrt(SEQ)
    tw_ph = 2.0 * np.pi * (n[:, None] * n[None, :]) / SEQ
    twr, twi = np.cos(tw_ph), -np.sin(tw_ph)
    cs = np.concatenate([np.cos(ph), np.sin(ph)], axis=0) / math.sqrt(FNET_GROUP_CH)
    return (jnp.asarray(f_stack, F32).astype(BF16), jnp.asarray(fr, F32), jnp.asarray(fi, F32),
            jnp.asarray(twr, F32), jnp.asarray(twi, F32), jnp.asarray(cs, F32).astype(BF16))


def _fourier_mix(uf):
    f_stack, fr, fi, twr, twi, cs = _dft_tables()
    mr = fr[None] * twr[:, None, :] - fi[None] * twi[:, None, :]
    mi = fr[None] * twi[:, None, :] + fi[None] * twr[:, None, :]
    m_full = jnp.concatenate([jnp.concatenate([mr, -mi], axis=2),
                              jnp.concatenate([mi, mr], axis=2)], axis=1).astype(BF16)
    tr, ti = _fft1(uf.reshape(FFT_N1, FFT_N2 * D_B), f_stack)
    y2d = _fft2(tr.reshape(FFT_N1, FFT_N2, D_B), ti.reshape(FFT_N1, FFT_N2, D_B), m_full, cs)
    return y2d.reshape(SEQ, D_B)


TM_MLP = 512
TF_MLP = 1024


def _mlp_kernel(x_ref, ya_ref, yb_ref, wo_ref, g_ref, wup_ref, wdn_ref, gf_ref, o_ref, h_ref, *,
                final):
    f = pl.program_id(1)
    half = wo_ref.shape[0] // 2

    @pl.when(f == 0)
    def _():
        for c in range(0, D_MODEL, 512):
            o_ref[:, c:c + 512] = (x_ref[:, c:c + 512]
                                   + _dot(ya_ref[...], wo_ref[0:half, c:c + 512])
                                   + _dot(yb_ref[...], wo_ref[half:, c:c + 512]))
        h_ref[...] = _rms(o_ref[...], g_ref[...]).astype(BF16)

    a = _dot(h_ref[...], wup_ref[...])
    a = jnp.square(jnp.maximum(a, 0.0)).astype(BF16)
    for c in range(0, D_MODEL, 512):
        o_ref[:, c:c + 512] += _dot(a, wdn_ref[:, c:c + 512])

    if final:
        @pl.when(f == pl.num_programs(1) - 1)
        def _():
            o_ref[...] = _rms(o_ref[...], gf_ref[...])


def _outproj_mlp(x, ya, yb, wo, g, wup, wdn, gf, layer, final):
    return pl.pallas_call(
        functools.partial(_mlp_kernel, final=final),
        grid=(SEQ // TM_MLP, D_FF // TF_MLP),
        in_specs=[pl.BlockSpec((TM_MLP, D_MODEL), lambda i, f: (i, 0)),
                  pl.BlockSpec((TM_MLP, ya.shape[1]), lambda i, f: (i, 0)),
                  pl.BlockSpec((TM_MLP, yb.shape[1]), lambda i, f: (i, 0)),
                  _resident(wo.shape),
                  _resident((1, D_MODEL)),
                  pl.BlockSpec((None, D_MODEL, TF_MLP), lambda i, f: (layer, 0, f)),
                  pl.BlockSpec((None, TF_MLP, D_MODEL), lambda i, f: (layer, f, 0)),
                  _resident((1, D_MODEL))],
        out_specs=pl.BlockSpec((TM_MLP, D_MODEL), lambda i, f: (i, 0)),
        out_shape=jax.ShapeDtypeStruct((SEQ, D_MODEL), F32),
        scratch_shapes=[pltpu.VMEM((TM_MLP, D_MODEL), BF16)],
        compiler_params=_params(("arbitrary", "arbitrary")),
        name="outproj_mlp_final" if final else "outproj_mlp",
    )(x, ya, yb, wo, g, wup, wdn, gf)


TM_INO = 256
Q_COLS = 3 * LANES


def _inproj_o_kernel(x_ref, g_ref, w_ref, qg_ref, wuq_ref, kvg_ref, wukv_ref, cos_ref, sin_ref,
                     bg_ref, ch_ref, q_ref, k_ref, v_ref, *, qscale):
    h = _rms(x_ref[...], g_ref[...]).astype(BF16)
    for c in range(0, D_C, 512):
        bg_ref[:, c:c + 512] = _dot(h, w_ref[:, c:c + 512])
        ch_ref[:, c:c + 512] = (_dot(h, w_ref[:, D_C + c:D_C + c + 512])
                                * _dot(h, w_ref[:, 2 * D_C + c:2 * D_C + c + 512]))
    cosv = cos_ref[...]
    sinv = sin_ref[...]
    tail = lax.broadcasted_iota(jnp.int32, (V_ROWS - V_HEAD, x_ref.shape[0]), 0)
    ones_row = jnp.where(tail == 0, 1.0, 0.0).astype(BF16)
    lat = _dot(h, w_ref[:, 3 * D_C:])
    o = Q_LORA + KV_LORA
    kr = (lat[:, o:o + LANES] * cosv + lat[:, o + LANES:o + 2 * LANES] * sinv).astype(BF16)
    cq = _rms(lat[:, :Q_LORA], qg_ref[...]).astype(BF16)
    ckv = _rms(lat[:, Q_LORA:o], kvg_ref[...]).astype(BF16)
    for hd in range(MLA_HEADS):
        qh = _dot(cq, wuq_ref[:, hd * Q_COLS:(hd + 1) * Q_COLS])
        q_ref[hd, 0:LANES, :] = (qh[:, 0:LANES] * qscale).T.astype(BF16)
        q_ref[hd, LANES:, :] = ((qh[:, LANES:2 * LANES] * cosv + qh[:, 2 * LANES:] * sinv)
                                * qscale).T.astype(BF16)
        kv = _dot(ckv, wukv_ref[:, hd * 2 * LANES:(hd + 1) * 2 * LANES])
        k_ref[hd, :, 0:LANES] = kv[:, 0:LANES].astype(BF16)
        k_ref[hd, :, LANES:] = kr
        v_ref[hd, 0, 0:V_HEAD, :] = kv[:, LANES:].T.astype(BF16)
        v_ref[hd, 0, V_HEAD:, :] = ones_row


def _inproj_o(x, g, w, qg, wuq, kvg, wukv, cos_t, sin_t, qscale):
    tm = TM_INO
    row = lambda i: (i, 0)
    head = lambda i: (0, i, 0)
    return pl.pallas_call(
        functools.partial(_inproj_o_kernel, qscale=qscale),
        grid=(SEQ // tm,),
        in_specs=[pl.BlockSpec((tm, D_MODEL), row),
                  _resident((1, D_MODEL)),
                  _resident(w.shape),
                  _resident((1, Q_LORA)), _resident(wuq.shape),
                  _resident((1, KV_LORA)), _resident(wukv.shape),
                  pl.BlockSpec((tm, LANES), row), pl.BlockSpec((tm, LANES), row)],
        out_specs=[pl.BlockSpec((tm, D_C), row), pl.BlockSpec((tm, D_C), row),
                   pl.BlockSpec((MLA_HEADS, QK_PAD, tm), lambda i: (0, 0, i)),
                   pl.BlockSpec((MLA_HEADS, tm, QK_PAD), head),
                   pl.BlockSpec((MLA_HEADS, 1, V_ROWS, tm),
                                lambda i: (0, i // (TK_ATT // tm), 0, i % (TK_ATT // tm)))],
        out_shape=[jax.ShapeDtypeStruct((SEQ, D_C), F32), jax.ShapeDtypeStruct((SEQ, D_C), F32),
                   jax.ShapeDtypeStruct((MLA_HEADS, QK_PAD, SEQ), BF16),
                   jax.ShapeDtypeStruct((MLA_HEADS, SEQ, QK_PAD), BF16),
                   jax.ShapeDtypeStruct((MLA_HEADS, SEQ // TK_ATT, V_ROWS, TK_ATT), BF16)],
        compiler_params=_params(("arbitrary",)),
        name="inproj_o",
    )(x, g, w, qg, wuq, kvg, wukv, cos_t, sin_t)


TS_SC = 256
HALO_SC = 8


def _shortconv_kernel(prev_ref, main_ref, next_ref, bg_ref, w_ref, o_ref, ext_ref):
    i = pl.program_id(0)
    n = pl.num_programs(0)
    ext_ref[0:HALO_SC, :] = jnp.where(i > 0, prev_ref[...], 0.0)
    ext_ref[HALO_SC:HALO_SC + TS_SC, :] = main_ref[...]
    ext_ref[HALO_SC + TS_SC:, :] = jnp.where(i < n - 1, next_ref[...], 0.0)
    for c in range(0, D_C, 256):
        acc = w_ref[0:1, c:c + 256] * ext_ref[HALO_SC - 1:HALO_SC - 1 + TS_SC, c:c + 256]
        acc = acc + w_ref[1:2, c:c + 256] * ext_ref[HALO_SC:HALO_SC + TS_SC, c:c + 256]
        acc = acc + w_ref[2:3, c:c + 256] * ext_ref[HALO_SC + 1:HALO_SC + 1 + TS_SC, c:c + 256]
        o_ref[:, c:c + 256] = (bg_ref[:, c:c + 256] * acc).astype(BF16)


def _shortconv(bg, ch, w):
    r = TS_SC // HALO_SC
    nh = SEQ // HALO_SC
    return pl.pallas_call(
        _shortconv_kernel,
        grid=(SEQ // TS_SC,),
        in_specs=[pl.BlockSpec((HALO_SC, D_C), lambda i: (jnp.maximum(i * r - 1, 0), 0)),
                  pl.BlockSpec((TS_SC, D_C), lambda i: (i, 0)),
                  pl.BlockSpec((HALO_SC, D_C), lambda i: (jnp.minimum((i + 1) * r, nh - 1), 0)),
                  pl.BlockSpec((TS_SC, D_C), lambda i: (i, 0)),
                  _resident((3, D_C))],
        out_specs=pl.BlockSpec((TS_SC, D_C), lambda i: (i, 0)),
        out_shape=jax.ShapeDtypeStruct((SEQ, D_C), BF16),
        scratch_shapes=[pltpu.VMEM((TS_SC + 2 * HALO_SC, D_C), F32)],
        compiler_params=_params(("arbitrary",)),
        name="shortconv",
    )(ch, ch, ch, bg, w)


TQ_ATT = 1024
TK_ATT = 512
NKB_ATT = SEQ // TK_ATT
LW_ATT = 256
UNROLL_ATT = 4


def _attn_kernel(q_ref, k_ref, v_ref, o_ref, s0, s1, p0, p1, a0, a1, mb0, mb1, m_ref, acc_ref):
    chunks = [slice(c, c + LW_ATT) for c in range(0, TQ_ATT, LW_ATT)]

    def scores(kb, s_ref, mb_ref, sl):
        off = pl.multiple_of(kb * TK_ATT, TK_ATT)
        st = _dot(k_ref[0, pl.ds(off, TK_ATT), :], q_ref[0, :, sl])
        s_ref[:, sl] = st
        mb_ref[:, sl] = jnp.max(st, axis=0, keepdims=True)

    def softmax(s_ref, mb_ref, p_ref, a_ref, sl):
        m_prev = m_ref[:, sl]
        m_new = jnp.maximum(m_prev, mb_ref[:, sl])
        a_ref[:, sl] = jnp.exp2(m_prev - m_new)
        m_ref[:, sl] = m_new
        p_ref[:, sl] = jnp.exp2(s_ref[:, sl] - m_new).astype(BF16)

    def values(kb, p_ref, a_ref, sl):
        acc_ref[:, sl] = a_ref[:, sl] * acc_ref[:, sl] + _dot(v_ref[0, kb], p_ref[:, sl])

    def stage(kb_scores, s_w, mb_w, s_r, mb_r, p_w, a_w, kb_values, p_r, a_r):
        for sl in chunks:
            scores(kb_scores, s_w, mb_w, sl)
            softmax(s_r, mb_r, p_w, a_w, sl)
            values(kb_values, p_r, a_r, sl)

    m_ref[...] = jnp.full(m_ref.shape, -jnp.inf, F32)
    acc_ref[...] = jnp.zeros(acc_ref.shape, F32)
    p1[...] = jnp.zeros(p1.shape, BF16)
    a1[...] = jnp.ones(a1.shape, F32)
    for sl in chunks:
        scores(0, s0, mb0, sl)

    def body(i, carry):
        for u in range(0, UNROLL_ATT, 2):
            j = UNROLL_ATT * i + u
            stage(j + 1, s1, mb1, s0, mb0, p0, a0, jnp.maximum(j - 1, 0), p1, a1)
            stage(jnp.minimum(j + 2, NKB_ATT - 1), s0, mb0, s1, mb1, p1, a1, j, p0, a0)
        return carry

    lax.fori_loop(0, NKB_ATT // UNROLL_ATT, body, 0)
    for sl in chunks:
        values(NKB_ATT - 1, p1, a1, sl)
    o_ref[...] = (acc_ref[0:V_HEAD, :] / acc_ref[V_HEAD:V_HEAD + 1, :]).T.astype(BF16)


def _attention(qt, k, vt):
    return pl.pallas_call(
        _attn_kernel,
        grid=(MLA_HEADS, SEQ // TQ_ATT),
        in_specs=[pl.BlockSpec((1, QK_PAD, TQ_ATT), lambda h, i: (h, 0, i)),
                  pl.BlockSpec((1, SEQ, QK_PAD), lambda h, i: (h, 0, 0)),
                  pl.BlockSpec((1, SEQ // TK_ATT, V_ROWS, TK_ATT), lambda h, i: (h, 0, 0, 0))],
        out_specs=pl.BlockSpec((TQ_ATT, V_HEAD), lambda h, i: (i, h)),
        out_shape=jax.ShapeDtypeStruct((SEQ, D_ATT), BF16),
        scratch_shapes=[pltpu.VMEM((TK_ATT, TQ_ATT), F32), pltpu.VMEM((TK_ATT, TQ_ATT), F32),
                        pltpu.VMEM((TK_ATT, TQ_ATT), BF16), pltpu.VMEM((TK_ATT, TQ_ATT), BF16),
                        pltpu.VMEM((1, TQ_ATT), F32), pltpu.VMEM((1, TQ_ATT), F32),
                        pltpu.VMEM((1, TQ_ATT), F32), pltpu.VMEM((1, TQ_ATT), F32),
                        pltpu.VMEM((1, TQ_ATT), F32),
                        pltpu.VMEM((V_ROWS, TQ_ATT), F32)],
        compiler_params=_params(("arbitrary", "arbitrary")),
        name="mla_attention",
    )(qt, k, vt)


def _rope_tiles(w_rope):
    half = QK_ROPE // 2
    z = jnp.zeros((w_rope.shape[0], LANES - QK_ROPE), w_rope.dtype)
    a = jnp.concatenate([w_rope, z], axis=1)
    b = jnp.concatenate([w_rope[:, half:], w_rope[:, :half], z], axis=1)
    return a, b


def _rope_tables():
    inv = 1.0 / (ROPE_THETA ** (jnp.arange(0, QK_ROPE, 2, dtype=F32) / QK_ROPE))
    ang = jnp.arange(SEQ, dtype=F32)[:, None] * inv[None, :]
    cos, sin = jnp.cos(ang), jnp.sin(ang)
    z = jnp.zeros((SEQ, LANES - QK_ROPE), F32)
    return (jnp.concatenate([cos, cos, z], axis=1), jnp.concatenate([-sin, sin, z], axis=1))


def kernel(x, mix_norm_e, w_in_e, conv_a_w, conv_a_b, ln_a_g, ln_a_b, w_out_e, mix_norm_o, w_in_o,
           conv_c_w, q_norm_g, w_uq, kv_norm_g, w_ukv, w_out_o, mlp_norm, w_up, w_down, final_norm):
    xs = x[0]
    row = lambda v: v.reshape(1, -1)

    hglu, uf = _inproj_e(xs, row(mix_norm_e[0]), w_in_e[0].astype(BF16))
    ya = _conformer(hglu, conv_a_w[0], row(conv_a_b[0]), row(ln_a_g[0]), row(ln_a_b[0]))
    yb = _fourier_mix(uf)
    w_up_b = w_up.astype(BF16)
    w_down_b = w_down.astype(BF16)
    xs = _outproj_mlp(xs, ya, yb, w_out_e[0].astype(BF16), row(mlp_norm[0]),
                      w_up_b, w_down_b, row(final_norm), layer=0, final=False)

    wi = w_in_o[0]
    o = 3 * D_C + Q_LORA + KV_LORA
    kra, krb = _rope_tiles(wi[:, o:])
    w_in = jnp.concatenate([wi[:, :o], kra, krb], axis=1).astype(BF16)
    wq = w_uq[0].reshape(Q_LORA, MLA_HEADS, QK_HEAD)
    qa, qb = _rope_tiles(wq[:, :, QK_NOPE:].reshape(Q_LORA * MLA_HEADS, QK_ROPE))
    wuq = jnp.concatenate([wq[:, :, :QK_NOPE], qa.reshape(Q_LORA, MLA_HEADS, LANES),
                           qb.reshape(Q_LORA, MLA_HEADS, LANES)], axis=2)
    wuq = wuq.reshape(Q_LORA, MLA_HEADS * Q_COLS).astype(BF16)
    cos_t, sin_t = _rope_tables()
    qscale = (QK_HEAD ** -0.5) * math.log2(math.e)
    bg, ch, q, k, v = _inproj_o(xs, row(mix_norm_o[0]), w_in,
                                row(q_norm_g[0]), wuq, row(kv_norm_g[0]), w_ukv[0].astype(BF16),
                                cos_t, sin_t, qscale)
    yc = _shortconv(bg, ch, conv_c_w[0])
    yd = _attention(q, k, v)
    xs = _outproj_mlp(xs, yc, yd, w_out_o[0].astype(BF16), row(mlp_norm[1]),
                      w_up_b, w_down_b, row(final_norm), layer=1, final=True)
    return xs[None]
```

```python
import functools
import math

import numpy as np
import jax
import jax.numpy as jnp
from jax import lax
from jax.experimental import pallas as pl
from jax.experimental.pallas import tpu as pltpu

D_MODEL = 2048
SEQ = 16384
D_A = 1024
CONF_KERNEL = 31
D_B = 1024
FNET_GROUPS = 8
FNET_GROUP_CH = 128
D_C = 1024
MLA_HEADS = 8
Q_LORA = 512
KV_LORA = 256
QK_NOPE = 128
QK_ROPE = 64
V_HEAD = 128
QK_HEAD = QK_NOPE + QK_ROPE
D_ATT = MLA_HEADS * V_HEAD
ROPE_THETA = 10000.0
D_FF = 4 * D_MODEL
EPS = 1e-6

LANES = 128
QK_PAD = 2 * LANES
V_ROWS = V_HEAD + 16
FFT_N1 = 128
FFT_N2 = 128
VMEM_LIMIT = 56 * 1024 * 1024

BF16 = jnp.bfloat16
F32 = jnp.float32


def _dot(a, b):
    return jnp.dot(a, b, preferred_element_type=F32)


def _rms(x, g):
    return x * lax.rsqrt(jnp.mean(x * x, axis=-1, keepdims=True) + EPS) * g


def _params(sem):
    return pltpu.CompilerParams(dimension_semantics=sem, vmem_limit_bytes=VMEM_LIMIT)


def _resident(shape):
    nd = len(shape)
    return pl.BlockSpec(shape, lambda *_: (0,) * nd, pipeline_mode=pl.Buffered(1))


TM_IN = 512


def _inproj_e_kernel(x_ref, g_ref, w_ref, glu_ref, uf_ref):
    h = _rms(x_ref[...], g_ref[...]).astype(BF16)
    for c in range(0, D_A, 512):
        val = _dot(h, w_ref[:, c:c + 512])
        gate = _dot(h, w_ref[:, D_A + c:D_A + c + 512])
        glu_ref[:, c:c + 512] = val * jax.nn.sigmoid(gate)
    for c in range(0, D_B, 512):
        uf_ref[:, c:c + 512] = _dot(h, w_ref[:, 2 * D_A + c:2 * D_A + c + 512]).astype(BF16)


def _inproj_e(x, g, w):
    return pl.pallas_call(
        _inproj_e_kernel,
        grid=(SEQ // TM_IN,),
        in_specs=[pl.BlockSpec((TM_IN, D_MODEL), lambda i: (i, 0)),
                  _resident((1, D_MODEL)),
                  _resident((D_MODEL, 2 * D_A + D_B))],
        out_specs=[pl.BlockSpec((TM_IN, D_A), lambda i: (i, 0)),
                   pl.BlockSpec((TM_IN, D_B), lambda i: (i, 0))],
        out_shape=[jax.ShapeDtypeStruct((SEQ, D_A), F32),
                   jax.ShapeDtypeStruct((SEQ, D_B), BF16)],
        compiler_params=_params(("arbitrary",)),
        name="inproj_e",
    )(x, g, w)


TC_CONF = 128
HALO_CONF = 16


SUBLANES = 8
CONF_Q = (-2, -1, 0, 1, 2)


def _conf_tap_plan():
    plan = np.full((len(CONF_Q), SUBLANES, SUBLANES), -1, np.int32)
    for qi, q in enumerate(CONF_Q):
        for r in range(SUBLANES):
            for i in range(SUBLANES):
                d = 8 * q + r if i + r < SUBLANES else 8 * (q - 1) + r
                if abs(d) <= CONF_KERNEL // 2:
                    plan[qi, r, i] = d + CONF_KERNEL // 2
    return plan


def _conformer_kernel(prev_ref, main_ref, next_ref, w_ref, b_ref, lg_ref, lb_ref, o_ref,
                      hext_ref, conv_ref, *, live):
    i = pl.program_id(0)
    n = pl.num_programs(0)
    hext_ref[0:HALO_CONF, :] = jnp.where(i > 0, prev_ref[...], 0.0)
    hext_ref[HALO_CONF:HALO_CONF + TC_CONF, :] = main_ref[...]
    hext_ref[HALO_CONF + TC_CONF:, :] = jnp.where(i < n - 1, next_ref[...], 0.0)
    nt_out = TC_CONF // SUBLANES
    nt_ext = (TC_CONF + 2 * HALO_CONF) // SUBLANES
    t0 = HALO_CONF // SUBLANES
    for c in range(0, D_A, LANES):
        x3 = hext_ref[:, c:c + LANES].reshape(nt_ext, SUBLANES, LANES)
        acc = jnp.zeros((nt_out, SUBLANES, LANES), F32) + b_ref[:, c:c + LANES]
        for r in range(SUBLANES):
            xr = x3 if r == 0 else pltpu.roll(x3, SUBLANES - r, axis=1)
            for qi, q in enumerate(CONF_Q):
                if live[qi][r]:
                    row = (qi * SUBLANES + r) * SUBLANES
                    acc = acc + w_ref[row:row + SUBLANES, c:c + LANES] * xr[t0 + q:t0 + q + nt_out]
        conv_ref[:, c:c + LANES] = acc.reshape(TC_CONF, LANES)
    y = conv_ref[...]
    mu = jnp.mean(y, axis=-1, keepdims=True)
    d = y - mu
    var = jnp.mean(d * d, axis=-1, keepdims=True)
    z = d * lax.rsqrt(var + EPS) * lg_ref[...] + lb_ref[...]
    o_ref[...] = (z * jax.nn.sigmoid(z)).astype(BF16)


def _conformer(hglu, w, b, lg, lb):
    r = TC_CONF // HALO_CONF
    nh = SEQ // HALO_CONF
    plan = _conf_tap_plan()
    live = tuple(tuple(bool((plan[qi, rr] >= 0).any()) for rr in range(SUBLANES))
                 for qi in range(len(CONF_Q)))
    wtab = jnp.where((plan >= 0)[..., None], w[np.maximum(plan, 0)], 0.0).reshape(-1, D_A)
    return pl.pallas_call(
        functools.partial(_conformer_kernel, live=live),
        grid=(SEQ // TC_CONF,),
        in_specs=[pl.BlockSpec((HALO_CONF, D_A), lambda i: (jnp.maximum(i * r - 1, 0), 0)),
                  pl.BlockSpec((TC_CONF, D_A), lambda i: (i, 0)),
                  pl.BlockSpec((HALO_CONF, D_A), lambda i: (jnp.minimum((i + 1) * r, nh - 1), 0)),
                  _resident((len(CONF_Q) * SUBLANES * SUBLANES, D_A)),
                  _resident((1, D_A)), _resident((1, D_A)), _resident((1, D_A))],
        out_specs=pl.BlockSpec((TC_CONF, D_A), lambda i: (i, 0)),
        out_shape=jax.ShapeDtypeStruct((SEQ, D_A), BF16),
        scratch_shapes=[pltpu.VMEM((TC_CONF + 2 * HALO_CONF, D_A), F32),
                        pltpu.VMEM((TC_CONF, D_A), F32)],
        compiler_params=_params(("arbitrary",)),
        name="conformer",
    )(hglu, hglu, hglu, wtab, b, lg, lb)


TN_FFT1 = 8192
TK1_FFT2 = 8


def _fft1_kernel(x_ref, f_ref, tr_ref, ti_ref):
    for c in range(0, TN_FFT1, 1024):
        t = _dot(f_ref[...], x_ref[:, c:c + 1024])
        tr_ref[:, c:c + 1024] = t[:FFT_N1].astype(BF16)
        ti_ref[:, c:c + 1024] = t[FFT_N1:].astype(BF16)


def _fft1(x2d, f_stack):
    ncol = FFT_N2 * D_B
    return pl.pallas_call(
        _fft1_kernel,
        grid=(ncol // TN_FFT1,),
        in_specs=[pl.BlockSpec((FFT_N1, TN_FFT1), lambda j: (0, j)),
                  _resident((2 * FFT_N1, FFT_N1))],
        out_specs=[pl.BlockSpec((FFT_N1, TN_FFT1), lambda j: (0, j)),
                   pl.BlockSpec((FFT_N1, TN_FFT1), lambda j: (0, j))],
        out_shape=[jax.ShapeDtypeStruct((FFT_N1, ncol), BF16),
                   jax.ShapeDtypeStruct((FFT_N1, ncol), BF16)],
        compiler_params=_params(("arbitrary",)),
        name="fft_stage1",
    )(x2d, f_stack)


def _fft2_kernel(tr_ref, ti_ref, m_ref, cs_ref, o_ref):
    for j in range(TK1_FFT2):
        t = jnp.concatenate([tr_ref[j], ti_ref[j]], axis=0)
        z = _dot(m_ref[j], t)
        zr = z[:FFT_N2].astype(BF16)
        zi = z[FFT_N2:].astype(BF16)
        for g in range(FNET_GROUPS):
            lo = g * FNET_GROUP_CH
            zz = jnp.concatenate([zr[:, lo:lo + FNET_GROUP_CH], zi[:, lo:lo + FNET_GROUP_CH]], axis=1)
            o_ref[:, j * D_B + lo:j * D_B + lo + FNET_GROUP_CH] = _dot(zz, cs_ref[...]).astype(BF16)


def _fft2(tr3, ti3, m_full, cs):
    return pl.pallas_call(
        _fft2_kernel,
        grid=(FFT_N1 // TK1_FFT2,),
        in_specs=[pl.BlockSpec((TK1_FFT2, FFT_N2, D_B), lambda i: (i, 0, 0)),
                  pl.BlockSpec((TK1_FFT2, FFT_N2, D_B), lambda i: (i, 0, 0)),
                  pl.BlockSpec((TK1_FFT2, 2 * FFT_N2, 2 * FFT_N2), lambda i: (i, 0, 0)),
                  _resident((2 * FNET_GROUP_CH, FNET_GROUP_CH))],
        out_specs=pl.BlockSpec((FFT_N2, TK1_FFT2 * D_B), lambda i: (0, i)),
        out_shape=jax.ShapeDtypeStruct((FFT_N2, FFT_N1 * D_B), BF16),
        compiler_params=_params(("arbitrary",)),
        name="fft_stage2",
    )(tr3, ti3, m_full, cs)


def _dft_tables():
    n = np.arange(128)
    ph = 2.0 * np.pi * ((n[:, None] * n[None, :]) % 128) / 128.0
    fr, fi = np.cos(ph), -np.sin(ph)
    f_stack = np.concatenate([fr, fi], axis=0) / math.sqrt(SEQ)
    tw_ph = 2.0 * np.pi * (n[:, None] * n[None, :]) / SEQ
    twr, twi = np.cos(tw_ph), -np.sin(tw_ph)
    cs = np.concatenate([np.cos(ph), np.sin(ph)], axis=0) / math.sqrt(FNET_GROUP_CH)
    return (jnp.asarray(f_stack, F32).astype(BF16), jnp.asarray(fr, F32), jnp.asarray(fi, F32),
            jnp.asarray(twr, F32), jnp.asarray(twi, F32), jnp.asarray(cs, F32).astype(BF16))


def _fourier_mix(uf):
    f_stack, fr, fi, twr, twi, cs = _dft_tables()
    mr = fr[None] * twr[:, None, :] - fi[None] * twi[:, None, :]
    mi = fr[None] * twi[:, None, :] + fi[None] * twr[:, None, :]
    m_full = jnp.concatenate([jnp.concatenate([mr, -mi], axis=2),
                              jnp.concatenate([mi, mr], axis=2)], axis=1).astype(BF16)
    tr, ti = _fft1(uf.reshape(FFT_N1, FFT_N2 * D_B), f_stack)
    y2d = _fft2(tr.reshape(FFT_N1, FFT_N2, D_B), ti.reshape(FFT_N1, FFT_N2, D_B), m_full, cs)
    return y2d.reshape(SEQ, D_B)


TM_MLP = 512
TF_MLP = 1024


def _mlp_kernel(x_ref, ya_ref, yb_ref, wo_ref, g_ref, wup_ref, wdn_ref, gf_ref, o_ref, h_ref, *,
                final):
    f = pl.program_id(1)
    half = wo_ref.shape[0] // 2

    @pl.when(f == 0)
    def _():
        for c in range(0, D_MODEL, 512):
            o_ref[:, c:c + 512] = (x_ref[:, c:c + 512]
                                   + _dot(ya_ref[...], wo_ref[0:half, c:c + 512])
                                   + _dot(yb_ref[...], wo_ref[half:, c:c + 512]))
        h_ref[...] = _rms(o_ref[...], g_ref[...]).astype(BF16)

    a = _dot(h_ref[...], wup_ref[...])
    a = jnp.square(jnp.maximum(a, 0.0)).astype(BF16)
    for c in range(0, D_MODEL, 512):
        o_ref[:, c:c + 512] += _dot(a, wdn_ref[:, c:c + 512])

    if final:
        @pl.when(f == pl.num_programs(1) - 1)
        def _():
            o_ref[...] = _rms(o_ref[...], gf_ref[...])


def _outproj_mlp(x, ya, yb, wo, g, wup, wdn, gf, layer, final):
    return pl.pallas_call(
        functools.partial(_mlp_kernel, final=final),
        grid=(SEQ // TM_MLP, D_FF // TF_MLP),
        in_specs=[pl.BlockSpec((TM_MLP, D_MODEL), lambda i, f: (i, 0)),
                  pl.BlockSpec((TM_MLP, ya.shape[1]), lambda i, f: (i, 0)),
                  pl.BlockSpec((TM_MLP, yb.shape[1]), lambda i, f: (i, 0)),
                  _resident(wo.shape),
                  _resident((1, D_MODEL)),
                  pl.BlockSpec((None, D_MODEL, TF_MLP), lambda i, f: (layer, 0, f)),
                  pl.BlockSpec((None, TF_MLP, D_MODEL), lambda i, f: (layer, f, 0)),
                  _resident((1, D_MODEL))],
        out_specs=pl.BlockSpec((TM_MLP, D_MODEL), lambda i, f: (i, 0)),
        out_shape=jax.ShapeDtypeStruct((SEQ, D_MODEL), F32),
        scratch_shapes=[pltpu.VMEM((TM_MLP, D_MODEL), BF16)],
        compiler_params=_params(("arbitrary", "arbitrary")),
        name="outproj_mlp_final" if final else "outproj_mlp",
    )(x, ya, yb, wo, g, wup, wdn, gf)


TM_INO = 256
Q_COLS = 3 * LANES


def _inproj_o_kernel(x_ref, g_ref, w_ref, qg_ref, wuq_ref, kvg_ref, wukv_ref, cos_ref, sin_ref,
                     bg_ref, ch_ref, q_ref, k_ref, v_ref, *, qscale):
    h = _rms(x_ref[...], g_ref[...]).astype(BF16)
    for c in range(0, D_C, 512):
        bg_ref[:, c:c + 512] = _dot(h, w_ref[:, c:c + 512])
        ch_ref[:, c:c + 512] = (_dot(h, w_ref[:, D_C + c:D_C + c + 512])
                                * _dot(h, w_ref[:, 2 * D_C + c:2 * D_C + c + 512]))
    cosv = cos_ref[...]
    sinv = sin_ref[...]
    tail = lax.broadcasted_iota(jnp.int32, (V_ROWS - V_HEAD, x_ref.shape[0]), 0)
    ones_row = jnp.where(tail == 0, 1.0, 0.0).astype(BF16)
    lat = _dot(h, w_ref[:, 3 * D_C:])
    o = Q_LORA + KV_LORA
    kr = (lat[:, o:o + LANES] * cosv + lat[:, o + LANES:o + 2 * LANES] * sinv).astype(BF16)
    cq = _rms(lat[:, :Q_LORA], qg_ref[...]).astype(BF16)
    ckv = _rms(lat[:, Q_LORA:o], kvg_ref[...]).astype(BF16)
    for hd in range(MLA_HEADS):
        qh = _dot(cq, wuq_ref[:, hd * Q_COLS:(hd + 1) * Q_COLS])
        q_ref[hd, 0:LANES, :] = (qh[:, 0:LANES] * qscale).T.astype(BF16)
        q_ref[hd, LANES:, :] = ((qh[:, LANES:2 * LANES] * cosv + qh[:, 2 * LANES:] * sinv)
                                * qscale).T.astype(BF16)
        kv = _dot(ckv, wukv_ref[:, hd * 2 * LANES:(hd + 1) * 2 * LANES])
        k_ref[hd, :, 0:LANES] = kv[:, 0:LANES].astype(BF16)
        k_ref[hd, :, LANES:] = kr
        v_ref[hd, 0, 0:V_HEAD, :] = kv[:, LANES:].T.astype(BF16)
        v_ref[hd, 0, V_HEAD:, :] = ones_row


def _inproj_o(x, g, w, qg, wuq, kvg, wukv, cos_t, sin_t, qscale):
    tm = TM_INO
    row = lambda i: (i, 0)
    head = lambda i: (0, i, 0)
    return pl.pallas_call(
        functools.partial(_inproj_o_kernel, qscale=qscale),
        grid=(SEQ // tm,),
        in_specs=[pl.BlockSpec((tm, D_MODEL), row),
                  _resident((1, D_MODEL)),
                  _resident(w.shape),
                  _resident((1, Q_LORA)), _resident(wuq.shape),
                  _resident((1, KV_LORA)), _resident(wukv.shape),
                  pl.BlockSpec((tm, LANES), row), pl.BlockSpec((tm, LANES), row)],
        out_specs=[pl.BlockSpec((tm, D_C), row), pl.BlockSpec((tm, D_C), row),
                   pl.BlockSpec((MLA_HEADS, QK_PAD, tm), lambda i: (0, 0, i)),
                   pl.BlockSpec((MLA_HEADS, tm, QK_PAD), head),
                   pl.BlockSpec((MLA_HEADS, 1, V_ROWS, tm),
                                lambda i: (0, i // (TK_ATT // tm), 0, i % (TK_ATT // tm)))],
        out_shape=[jax.ShapeDtypeStruct((SEQ, D_C), F32), jax.ShapeDtypeStruct((SEQ, D_C), F32),
                   jax.ShapeDtypeStruct((MLA_HEADS, QK_PAD, SEQ), BF16),
                   jax.ShapeDtypeStruct((MLA_HEADS, SEQ, QK_PAD), BF16),
                   jax.ShapeDtypeStruct((MLA_HEADS, SEQ // TK_ATT, V_ROWS, TK_ATT), BF16)],
        compiler_params=_params(("arbitrary",)),
        name="inproj_o",
    )(x, g, w, qg, wuq, kvg, wukv, cos_t, sin_t)


TS_SC = 256
HALO_SC = 8


def _shortconv_kernel(prev_ref, main_ref, next_ref, bg_ref, w_ref, o_ref, ext_ref):
    i = pl.program_id(0)
    n = pl.num_programs(0)
    ext_ref[0:HALO_SC, :] = jnp.where(i > 0, prev_ref[...], 0.0)
    ext_ref[HALO_SC:HALO_SC + TS_SC, :] = main_ref[...]
    ext_ref[HALO_SC + TS_SC:, :] = jnp.where(i < n - 1, next_ref[...], 0.0)
    for c in range(0, D_C, 256):
        acc = w_ref[0:1, c:c + 256] * ext_ref[HALO_SC - 1:HALO_SC - 1 + TS_SC, c:c + 256]
        acc = acc + w_ref[1:2, c:c + 256] * ext_ref[HALO_SC:HALO_SC + TS_SC, c:c + 256]
        acc = acc + w_ref[2:3, c:c + 256] * ext_ref[HALO_SC + 1:HALO_SC + 1 + TS_SC, c:c + 256]
        o_ref[:, c:c + 256] = (bg_ref[:, c:c + 256] * acc).astype(BF16)


def _shortconv(bg, ch, w):
    r = TS_SC // HALO_SC
    nh = SEQ // HALO_SC
    return pl.pallas_call(
        _shortconv_kernel,
        grid=(SEQ // TS_SC,),
        in_specs=[pl.BlockSpec((HALO_SC, D_C), lambda i: (jnp.maximum(i * r - 1, 0), 0)),
                  pl.BlockSpec((TS_SC, D_C), lambda i: (i, 0)),
                  pl.BlockSpec((HALO_SC, D_C), lambda i: (jnp.minimum((i + 1) * r, nh - 1), 0)),
                  pl.BlockSpec((TS_SC, D_C), lambda i: (i, 0)),
                  _resident((3, D_C))],
        out_specs=pl.BlockSpec((TS_SC, D_C), lambda i: (i, 0)),
        out_shape=jax.ShapeDtypeStruct((SEQ, D_C), BF16),
        scratch_shapes=[pltpu.VMEM((TS_SC + 2 * HALO_SC, D_C), F32)],
        compiler_params=_params(("arbitrary",)),
        name="shortconv",
    )(ch, ch, ch, bg, w)


TQ_ATT = 2048
TK_ATT = 512
NKB_ATT = SEQ // TK_ATT
LW_ATT = 256
UNROLL_ATT = 8


def _attn_kernel(q_ref, k_ref, v_ref, o_ref, s0, s1, p0, p1, a0, a1, mb0, mb1, m_ref, acc_ref):
    chunks = [slice(c, c + LW_ATT) for c in range(0, TQ_ATT, LW_ATT)]

    def scores(kb, s_ref, mb_ref, sl):
        off = pl.multiple_of(kb * TK_ATT, TK_ATT)
        st = _dot(k_ref[0, pl.ds(off, TK_ATT), :], q_ref[0, :, sl])
        s_ref[:, sl] = st
        mb_ref[:, sl] = jnp.max(st, axis=0, keepdims=True)

    def softmax(s_ref, mb_ref, p_ref, a_ref, sl):
        m_prev = m_ref[:, sl]
        m_new = jnp.maximum(m_prev, mb_ref[:, sl])
        a_ref[:, sl] = jnp.exp2(m_prev - m_new)
        m_ref[:, sl] = m_new
        p_ref[:, sl] = jnp.exp2(s_ref[:, sl] - m_new).astype(BF16)

    def values(kb, p_ref, a_ref, sl):
        acc_ref[:, sl] = a_ref[:, sl] * acc_ref[:, sl] + _dot(v_ref[0, kb], p_ref[:, sl])

    def stage(kb_scores, s_w, mb_w, s_r, mb_r, p_w, a_w, kb_values, p_r, a_r):
        for sl in chunks:
            scores(kb_scores, s_w, mb_w, sl)
            softmax(s_r, mb_r, p_w, a_w, sl)
            values(kb_values, p_r, a_r, sl)

    m_ref[...] = jnp.full(m_ref.shape, -jnp.inf, F32)
    acc_ref[...] = jnp.zeros(acc_ref.shape, F32)
    p1[...] = jnp.zeros(p1.shape, BF16)
    a1[...] = jnp.ones(a1.shape, F32)
    for sl in chunks:
        scores(0, s0, mb0, sl)

    def body(i, carry):
        for u in range(0, UNROLL_ATT, 2):
            j = UNROLL_ATT * i + u
            stage(j + 1, s1, mb1, s0, mb0, p0, a0, jnp.maximum(j - 1, 0), p1, a1)
            stage(jnp.minimum(j + 2, NKB_ATT - 1), s0, mb0, s1, mb1, p1, a1, j, p0, a0)
        return carry

    lax.fori_loop(0, NKB_ATT // UNROLL_ATT, body, 0)
    for sl in chunks:
        values(NKB_ATT - 1, p1, a1, sl)
    o_ref[...] = (acc_ref[0:V_HEAD, :] / acc_ref[V_HEAD:V_HEAD + 1, :]).T.astype(BF16)


def _attention(qt, k, vt):
    return pl.pallas_call(
        _attn_kernel,
        grid=(MLA_HEADS, SEQ // TQ_ATT),
        in_specs=[pl.BlockSpec((1, QK_PAD, TQ_ATT), lambda h, i: (h, 0, i)),
                  pl.BlockSpec((1, SEQ, QK_PAD), lambda h, i: (h, 0, 0)),
                  pl.BlockSpec((1, SEQ // TK_ATT, V_ROWS, TK_ATT), lambda h, i: (h, 0, 0, 0))],
        out_specs=pl.BlockSpec((TQ_ATT, V_HEAD), lambda h, i: (i, h)),
        out_shape=jax.ShapeDtypeStruct((SEQ, D_ATT), BF16),
        scratch_shapes=[pltpu.VMEM((TK_ATT, TQ_ATT), F32), pltpu.VMEM((TK_ATT, TQ_ATT), F32),
                        pltpu.VMEM((TK_ATT, TQ_ATT), BF16), pltpu.VMEM((TK_ATT, TQ_ATT), BF16),
                        pltpu.VMEM((1, TQ_ATT), F32), pltpu.VMEM((1, TQ_ATT), F32),
                        pltpu.VMEM((1, TQ_ATT), F32), pltpu.VMEM((1, TQ_ATT), F32),
                        pltpu.VMEM((1, TQ_ATT), F32),
                        pltpu.VMEM((V_ROWS, TQ_ATT), F32)],
        compiler_params=_params(("arbitrary", "arbitrary")),
        name="mla_attention",
    )(qt, k, vt)


def _rope_tiles(w_rope):
    half = QK_ROPE // 2
    z = jnp.zeros((w_rope.shape[0], LANES - QK_ROPE), w_rope.dtype)
    a = jnp.concatenate([w_rope, z], axis=1)
    b = jnp.concatenate([w_rope[:, half:], w_rope[:, :half], z], axis=1)
    return a, b


def _rope_tables():
    inv = 1.0 / (ROPE_THETA ** (jnp.arange(0, QK_ROPE, 2, dtype=F32) / QK_ROPE))
    ang = jnp.arange(SEQ, dtype=F32)[:, None] * inv[None, :]
    cos, sin = jnp.cos(ang), jnp.sin(ang)
    z = jnp.zeros((SEQ, LANES - QK_ROPE), F32)
    return (jnp.concatenate([cos, cos, z], axis=1), jnp.concatenate([-sin, sin, z], axis=1))


def kernel(x, mix_norm_e, w_in_e, conv_a_w, conv_a_b, ln_a_g, ln_a_b, w_out_e, mix_norm_o, w_in_o,
           conv_c_w, q_norm_g, w_uq, kv_norm_g, w_ukv, w_out_o, mlp_norm, w_up, w_down, final_norm):
    xs = x[0]
    row = lambda v: v.reshape(1, -1)

    hglu, uf = _inproj_e(xs, row(mix_norm_e[0]), w_in_e[0].astype(BF16))
    ya = _conformer(hglu, conv_a_w[0], row(conv_a_b[0]), row(ln_a_g[0]), row(ln_a_b[0]))
    yb = _fourier_mix(uf)
    w_up_b = w_up.astype(BF16)
    w_down_b = w_down.astype(BF16)
    xs = _outproj_mlp(xs, ya, yb, w_out_e[0].astype(BF16), row(mlp_norm[0]),
                      w_up_b, w_down_b, row(final_norm), layer=0, final=False)

    wi = w_in_o[0]
    o = 3 * D_C + Q_LORA + KV_LORA
    kra, krb = _rope_tiles(wi[:, o:])
    w_in = jnp.concatenate([wi[:, :o], kra, krb], axis=1).astype(BF16)
    wq = w_uq[0].reshape(Q_LORA, MLA_HEADS, QK_HEAD)
    qa, qb = _rope_tiles(wq[:, :, QK_NOPE:].reshape(Q_LORA * MLA_HEADS, QK_ROPE))
    wuq = jnp.concatenate([wq[:, :, :QK_NOPE], qa.reshape(Q_LORA, MLA_HEADS, LANES),
                           qb.reshape(Q_LORA, MLA_HEADS, LANES)], axis=2)
    wuq = wuq.reshape(Q_LORA, MLA_HEADS * Q_COLS).astype(BF16)
    cos_t, sin_t = _rope_tables()
    qscale = (QK_HEAD ** -0.5) * math.log2(math.e)
    bg, ch, q, k, v = _inproj_o(xs, row(mix_norm_o[0]), w_in,
                                row(q_norm_g[0]), wuq, row(kv_norm_g[0]), w_ukv[0].astype(BF16),
                                cos_t, sin_t, qscale)
    yc = _shortconv(bg, ch, conv_c_w[0])
    yd = _attention(q, k, v)
    xs = _outproj_mlp(xs, yc, yd, w_out_o[0].astype(BF16), row(mlp_norm[1]),
                      w_up_b, w_down_b, row(final_norm), layer=1, final=True)
    return xs[None]
```

```python
import functools
import math

import numpy as np
import jax
import jax.numpy as jnp
from jax import lax
from jax.experimental import pallas as pl
from jax.experimental.pallas import tpu as pltpu

D_MODEL = 2048
SEQ = 16384
D_A = 1024
CONF_KERNEL = 31
D_B = 1024
FNET_GROUPS = 8
FNET_GROUP_CH = 128
D_C = 1024
MLA_HEADS = 8
Q_LORA = 512
KV_LORA = 256
QK_NOPE = 128
QK_ROPE = 64
V_HEAD = 128
QK_HEAD = QK_NOPE + QK_ROPE
D_ATT = MLA_HEADS * V_HEAD
ROPE_THETA = 10000.0
D_FF = 4 * D_MODEL
EPS = 1e-6

LANES = 128
QK_PAD = 2 * LANES
V_ROWS = V_HEAD + 16
FFT_N1 = 128
FFT_N2 = 128
VMEM_LIMIT = 56 * 1024 * 1024

BF16 = jnp.bfloat16
F32 = jnp.float32


def _dot(a, b):
    return jnp.dot(a, b, preferred_element_type=F32)


def _rms(x, g):
    return x * lax.rsqrt(jnp.mean(x * x, axis=-1, keepdims=True) + EPS) * g


def _params(sem):
    return pltpu.CompilerParams(dimension_semantics=sem, vmem_limit_bytes=VMEM_LIMIT)


def _resident(shape):
    nd = len(shape)
    return pl.BlockSpec(shape, lambda *_: (0,) * nd, pipeline_mode=pl.Buffered(1))


TM_IN = 512


def _inproj_e_kernel(x_ref, g_ref, w_ref, glu_ref, uf_ref):
    h = _rms(x_ref[...], g_ref[...]).astype(BF16)
    for c in range(0, D_A, 512):
        val = _dot(h, w_ref[:, c:c + 512])
        gate = _dot(h, w_ref[:, D_A + c:D_A + c + 512])
        glu_ref[:, c:c + 512] = val * jax.nn.sigmoid(gate)
    for c in range(0, D_B, 512):
        uf_ref[:, c:c + 512] = _dot(h, w_ref[:, 2 * D_A + c:2 * D_A + c + 512]).astype(BF16)


def _inproj_e(x, g, w):
    return pl.pallas_call(
        _inproj_e_kernel,
        grid=(SEQ // TM_IN,),
        in_specs=[pl.BlockSpec((TM_IN, D_MODEL), lambda i: (i, 0)),
                  _resident((1, D_MODEL)),
                  _resident((D_MODEL, 2 * D_A + D_B))],
        out_specs=[pl.BlockSpec((TM_IN, D_A), lambda i: (i, 0)),
                   pl.BlockSpec((TM_IN, D_B), lambda i: (i, 0))],
        out_shape=[jax.ShapeDtypeStruct((SEQ, D_A), F32),
                   jax.ShapeDtypeStruct((SEQ, D_B), BF16)],
        compiler_params=_params(("arbitrary",)),
        name="inproj_e",
    )(x, g, w)


TC_CONF = 128
HALO_CONF = 16


SUBLANES = 8
CONF_Q = (-2, -1, 0, 1, 2)


def _conf_tap_plan():
    plan = np.full((len(CONF_Q), SUBLANES, SUBLANES), -1, np.int32)
    for qi, q in enumerate(CONF_Q):
        for r in range(SUBLANES):
            for i in range(SUBLANES):
                d = 8 * q + r if i + r < SUBLANES else 8 * (q - 1) + r
                if abs(d) <= CONF_KERNEL // 2:
                    plan[qi, r, i] = d + CONF_KERNEL // 2
    return plan


def _conformer_kernel(prev_ref, main_ref, next_ref, w_ref, b_ref, lg_ref, lb_ref, o_ref,
                      hext_ref, conv_ref, *, live):
    i = pl.program_id(0)
    n = pl.num_programs(0)
    hext_ref[0:HALO_CONF, :] = jnp.where(i > 0, prev_ref[...], 0.0)
    hext_ref[HALO_CONF:HALO_CONF + TC_CONF, :] = main_ref[...]
    hext_ref[HALO_CONF + TC_CONF:, :] = jnp.where(i < n - 1, next_ref[...], 0.0)
    nt_out = TC_CONF // SUBLANES
    nt_ext = (TC_CONF + 2 * HALO_CONF) // SUBLANES
    t0 = HALO_CONF // SUBLANES
    for c in range(0, D_A, LANES):
        x3 = hext_ref[:, c:c + LANES].reshape(nt_ext, SUBLANES, LANES)
        acc = jnp.zeros((nt_out, SUBLANES, LANES), F32) + b_ref[:, c:c + LANES]
        for r in range(SUBLANES):
            xr = x3 if r == 0 else pltpu.roll(x3, SUBLANES - r, axis=1)
            for qi, q in enumerate(CONF_Q):
                if live[qi][r]:
                    row = (qi * SUBLANES + r) * SUBLANES
                    acc = acc + w_ref[row:row + SUBLANES, c:c + LANES] * xr[t0 + q:t0 + q + nt_out]
        conv_ref[:, c:c + LANES] = acc.reshape(TC_CONF, LANES)
    y = conv_ref[...]
    mu = jnp.mean(y, axis=-1, keepdims=True)
    d = y - mu
    var = jnp.mean(d * d, axis=-1, keepdims=True)
    z = d * lax.rsqrt(var + EPS) * lg_ref[...] + lb_ref[...]
    o_ref[...] = (z * jax.nn.sigmoid(z)).astype(BF16)


def _conformer(hglu, w, b, lg, lb):
    r = TC_CONF // HALO_CONF
    nh = SEQ // HALO_CONF
    plan = _conf_tap_plan()
    live = tuple(tuple(bool((plan[qi, rr] >= 0).any()) for rr in range(SUBLANES))
                 for qi in range(len(CONF_Q)))
    wtab = jnp.where((plan >= 0)[..., None], w[np.maximum(plan, 0)], 0.0).reshape(-1, D_A)
    return pl.pallas_call(
        functools.partial(_conformer_kernel, live=live),
        grid=(SEQ // TC_CONF,),
        in_specs=[pl.BlockSpec((HALO_CONF, D_A), lambda i: (jnp.maximum(i * r - 1, 0), 0)),
                  pl.BlockSpec((TC_CONF, D_A), lambda i: (i, 0)),
                  pl.BlockSpec((HALO_CONF, D_A), lambda i: (jnp.minimum((i + 1) * r, nh - 1), 0)),
                  _resident((len(CONF_Q) * SUBLANES * SUBLANES, D_A)),
                  _resident((1, D_A)), _resident((1, D_A)), _resident((1, D_A))],
        out_specs=pl.BlockSpec((TC_CONF, D_A), lambda i: (i, 0)),
        out_shape=jax.ShapeDtypeStruct((SEQ, D_A), BF16),
        scratch_shapes=[pltpu.VMEM((TC_CONF + 2 * HALO_CONF, D_A), F32),
                        pltpu.VMEM((TC_CONF, D_A), F32)],
        compiler_params=_params(("arbitrary",)),
        name="conformer",
    )(hglu, hglu, hglu, wtab, b, lg, lb)


TN_FFT1 = 8192
TK1_FFT2 = 8


def _fft1_kernel(x_ref, f_ref, tr_ref, ti_ref):
    for c in range(0, TN_FFT1, 1024):
        t = _dot(f_ref[...], x_ref[:, c:c + 1024])
        tr_ref[:, c:c + 1024] = t[:FFT_N1].astype(BF16)
        ti_ref[:, c:c + 1024] = t[FFT_N1:].astype(BF16)


def _fft1(x2d, f_stack):
    ncol = FFT_N2 * D_B
    return pl.pallas_call(
        _fft1_kernel,
        grid=(ncol // TN_FFT1,),
        in_specs=[pl.BlockSpec((FFT_N1, TN_FFT1), lambda j: (0, j)),
                  _resident((2 * FFT_N1, FFT_N1))],
        out_specs=[pl.BlockSpec((FFT_N1, TN_FFT1), lambda j: (0, j)),
                   pl.BlockSpec((FFT_N1, TN_FFT1), lambda j: (0, j))],
        out_shape=[jax.ShapeDtypeStruct((FFT_N1, ncol), BF16),
                   jax.ShapeDtypeStruct((FFT_N1, ncol), BF16)],
        compiler_params=_params(("arbitrary",)),
        name="fft_stage1",
    )(x2d, f_stack)


def _fft2_kernel(tr_ref, ti_ref, m_ref, cs_ref, o_ref):
    for j in range(TK1_FFT2):
        t = jnp.concatenate([tr_ref[j], ti_ref[j]], axis=0)
        z = _dot(m_ref[j], t)
        zr = z[:FFT_N2].astype(BF16)
        zi = z[FFT_N2:].astype(BF16)
        for g in range(FNET_GROUPS):
            lo = g * FNET_GROUP_CH
            zz = jnp.concatenate([zr[:, lo:lo + FNET_GROUP_CH], zi[:, lo:lo + FNET_GROUP_CH]], axis=1)
            o_ref[:, j * D_B + lo:j * D_B + lo + FNET_GROUP_CH] = _dot(zz, cs_ref[...]).astype(BF16)


def _fft2(tr3, ti3, m_full, cs):
    return pl.pallas_call(
        _fft2_kernel,
        grid=(FFT_N1 // TK1_FFT2,),
        in_specs=[pl.BlockSpec((TK1_FFT2, FFT_N2, D_B), lambda i: (i, 0, 0)),
                  pl.BlockSpec((TK1_FFT2, FFT_N2, D_B), lambda i: (i, 0, 0)),
                  pl.BlockSpec((TK1_FFT2, 2 * FFT_N2, 2 * FFT_N2), lambda i: (i, 0, 0)),
                  _resident((2 * FNET_GROUP_CH, FNET_GROUP_CH))],
        out_specs=pl.BlockSpec((FFT_N2, TK1_FFT2 * D_B), lambda i: (0, i)),
        out_shape=jax.ShapeDtypeStruct((FFT_N2, FFT_N1 * D_B), BF16),
        compiler_params=_params(("arbitrary",)),
        name="fft_stage2",
    )(tr3, ti3, m_full, cs)


def _dft_tables():
    n = np.arange(128)
    ph = 2.0 * np.pi * ((n[:, None] * n[None, :]) % 128) / 128.0
    fr, fi = np.cos(ph), -np.sin(ph)
    f_stack = np.concatenate([fr, fi], axis=0) / math.sqrt(SEQ)
    tw_ph = 2.0 * np.pi * (n[:, None] * n[None, :]) / SEQ
    twr, twi = np.cos(tw_ph), -np.sin(tw_ph)
    cs = np.concatenate([np.cos(ph), np.sin(ph)], axis=0) / math.sqrt(FNET_GROUP_CH)
    return (jnp.asarray(f_stack, F32).astype(BF16), jnp.asarray(fr, F32), jnp.asarray(fi, F32),
            jnp.asarray(twr, F32), jnp.asarray(twi, F32), jnp.asarray(cs, F32).astype(BF16))


def _fourier_mix(uf):
    f_stack, fr, fi, twr, twi, cs = _dft_tables()
    mr = fr[None] * twr[:, None, :] - fi[None] * twi[:, None, :]
    mi = fr[None] * twi[:, None, :] + fi[None] * twr[:, None, :]
    m_full = jnp.concatenate([jnp.concatenate([mr, -mi], axis=2),
                              jnp.concatenate([mi, mr], axis=2)], axis=1).astype(BF16)
    tr, ti = _fft1(uf.reshape(FFT_N1, FFT_N2 * D_B), f_stack)
    y2d = _fft2(tr.reshape(FFT_N1, FFT_N2, D_B), ti.reshape(FFT_N1, FFT_N2, D_B), m_full, cs)
    return y2d.reshape(SEQ, D_B)


TM_MLP = 512
TF_MLP = 1024


def _mlp_kernel(x_ref, ya_ref, yb_ref, wo_ref, g_ref, wup_ref, wdn_ref, gf_ref, o_ref, h_ref, *,
                final):
    f = pl.program_id(1)
    half = wo_ref.shape[0] // 2

    @pl.when(f == 0)
    def _():
        for c in range(0, D_MODEL, 512):
            o_ref[:, c:c + 512] = (x_ref[:, c:c + 512]
                                   + _dot(ya_ref[...], wo_ref[0:half, c:c + 512])
                                   + _dot(yb_ref[...], wo_ref[half:, c:c + 512]))
        h_ref[...] = _rms(o_ref[...], g_ref[...]).astype(BF16)

    a = _dot(h_ref[...], wup_ref[...])
    a = jnp.square(jnp.maximum(a, 0.0)).astype(BF16)
    for c in range(0, D_MODEL, 512):
        o_ref[:, c:c + 512] += _dot(a, wdn_ref[:, c:c + 512])

    if final:
        @pl.when(f == pl.num_programs(1) - 1)
        def _():
            o_ref[...] = _rms(o_ref[...], gf_ref[...])


def _outproj_mlp(x, ya, yb, wo, g, wup, wdn, gf, layer, final):
    return pl.pallas_call(
        functools.partial(_mlp_kernel, final=final),
        grid=(SEQ // TM_MLP, D_FF // TF_MLP),
        in_specs=[pl.BlockSpec((TM_MLP, D_MODEL), lambda i, f: (i, 0)),
                  pl.BlockSpec((TM_MLP, ya.shape[1]), lambda i, f: (i, 0)),
                  pl.BlockSpec((TM_MLP, yb.shape[1]), lambda i, f: (i, 0)),
                  _resident(wo.shape),
                  _resident((1, D_MODEL)),
                  pl.BlockSpec((None, D_MODEL, TF_MLP), lambda i, f: (layer, 0, f)),
                  pl.BlockSpec((None, TF_MLP, D_MODEL), lambda i, f: (layer, f, 0)),
                  _resident((1, D_MODEL))],
        out_specs=pl.BlockSpec((TM_MLP, D_MODEL), lambda i, f: (i, 0)),
        out_shape=jax.ShapeDtypeStruct((SEQ, D_MODEL), F32),
        scratch_shapes=[pltpu.VMEM((TM_MLP, D_MODEL), BF16)],
        compiler_params=_params(("arbitrary", "arbitrary")),
        name="outproj_mlp_final" if final else "outproj_mlp",
    )(x, ya, yb, wo, g, wup, wdn, gf)


TM_INO = 256
Q_COLS = 3 * LANES


def _inproj_o_kernel(x_ref, g_ref, w_ref, qg_ref, wuq_ref, kvg_ref, wukv_ref, cos_ref, sin_ref,
                     bg_ref, ch_ref, q_ref, k_ref, v_ref, *, qscale):
    h = _rms(x_ref[...], g_ref[...]).astype(BF16)
    for c in range(0, D_C, 512):
        bg_ref[:, c:c + 512] = _dot(h, w_ref[:, c:c + 512])
        ch_ref[:, c:c + 512] = (_dot(h, w_ref[:, D_C + c:D_C + c + 512])
                                * _dot(h, w_ref[:, 2 * D_C + c:2 * D_C + c + 512]))
    cosv = cos_ref[...]
    sinv = sin_ref[...]
    tail = lax.broadcasted_iota(jnp.int32, (V_ROWS - V_HEAD, x_ref.shape[0]), 0)
    ones_row = jnp.where(tail == 0, 1.0, 0.0).astype(BF16)
    lat = _dot(h, w_ref[:, 3 * D_C:])
    o = Q_LORA + KV_LORA
    kr = (lat[:, o:o + LANES] * cosv + lat[:, o + LANES:o + 2 * LANES] * sinv).astype(BF16)
    cq = _rms(lat[:, :Q_LORA], qg_ref[...]).astype(BF16)
    ckv = _rms(lat[:, Q_LORA:o], kvg_ref[...]).astype(BF16)
    for hd in range(MLA_HEADS):
        qh = _dot(cq, wuq_ref[:, hd * Q_COLS:(hd + 1) * Q_COLS])
        q_ref[hd, 0:LANES, :] = (qh[:, 0:LANES] * qscale).T.astype(BF16)
        q_ref[hd, LANES:, :] = ((qh[:, LANES:2 * LANES] * cosv + qh[:, 2 * LANES:] * sinv)
                                * qscale).T.astype(BF16)
        kv = _dot(ckv, wukv_ref[:, hd * 2 * LANES:(hd + 1) * 2 * LANES])
        k_ref[hd, :, 0:LANES] = kv[:, 0:LANES].astype(BF16)
        k_ref[hd, :, LANES:] = kr
        v_ref[hd, 0, 0:V_HEAD, :] = kv[:, LANES:].T.astype(BF16)
        v_ref[hd, 0, V_HEAD:, :] = ones_row


def _inproj_o(x, g, w, qg, wuq, kvg, wukv, cos_t, sin_t, qscale):
    tm = TM_INO
    row = lambda i: (i, 0)
    head = lambda i: (0, i, 0)
    return pl.pallas_call(
        functools.partial(_inproj_o_kernel, qscale=qscale),
        grid=(SEQ // tm,),
        in_specs=[pl.BlockSpec((tm, D_MODEL), row),
                  _resident((1, D_MODEL)),
                  _resident(w.shape),
                  _resident((1, Q_LORA)), _resident(wuq.shape),
                  _resident((1, KV_LORA)), _resident(wukv.shape),
                  pl.BlockSpec((tm, LANES), row), pl.BlockSpec((tm, LANES), row)],
        out_specs=[pl.BlockSpec((tm, D_C), row), pl.BlockSpec((tm, D_C), row),
                   pl.BlockSpec((MLA_HEADS, QK_PAD, tm), lambda i: (0, 0, i)),
                   pl.BlockSpec((MLA_HEADS, tm, QK_PAD), head),
                   pl.BlockSpec((MLA_HEADS, 1, V_ROWS, tm),
                                lambda i: (0, i // (TK_ATT // tm), 0, i % (TK_ATT // tm)))],
        out_shape=[jax.ShapeDtypeStruct((SEQ, D_C), F32), jax.ShapeDtypeStruct((SEQ, D_C), F32),
                   jax.ShapeDtypeStruct((MLA_HEADS, QK_PAD, SEQ), BF16),
                   jax.ShapeDtypeStruct((MLA_HEADS, SEQ, QK_PAD), BF16),
                   jax.ShapeDtypeStruct((MLA_HEADS, SEQ // TK_ATT, V_ROWS, TK_ATT), BF16)],
        compiler_params=_params(("arbitrary",)),
        name="inproj_o",
    )(x, g, w, qg, wuq, kvg, wukv, cos_t, sin_t)


TS_SC = 256
HALO_SC = 8


def _shortconv_tile(t, n, prev_ref, main_ref, next_ref, bg_ref, w_ref, o_ref, ext_ref):
    ext_ref[0:HALO_SC, :] = jnp.where(t > 0, prev_ref[...], 0.0)
    ext_ref[HALO_SC:HALO_SC + TS_SC, :] = main_ref[...]
    ext_ref[HALO_SC + TS_SC:, :] = jnp.where(t < n - 1, next_ref[...], 0.0)
    for r0 in range(0, TS_SC, 128):
        for c in range(0, D_C, LANES):
            cs = slice(c, c + LANES)
            acc = w_ref[0:1, cs] * ext_ref[HALO_SC - 1 + r0:HALO_SC - 1 + r0 + 128, cs]
            acc = acc + w_ref[1:2, cs] * ext_ref[HALO_SC + r0:HALO_SC + r0 + 128, cs]
            acc = acc + w_ref[2:3, cs] * ext_ref[HALO_SC + 1 + r0:HALO_SC + 1 + r0 + 128, cs]
            o_ref[r0:r0 + 128, cs] = (bg_ref[r0:r0 + 128, cs] * acc).astype(BF16)


def _shortconv_specs(tile_of):
    r = TS_SC // HALO_SC
    nh = SEQ // HALO_SC
    ins = [pl.BlockSpec((HALO_SC, D_C), lambda *g: (jnp.maximum(tile_of(*g) * r - 1, 0), 0)),
           pl.BlockSpec((TS_SC, D_C), lambda *g: (tile_of(*g), 0)),
           pl.BlockSpec((HALO_SC, D_C), lambda *g: (jnp.minimum((tile_of(*g) + 1) * r, nh - 1), 0)),
           pl.BlockSpec((TS_SC, D_C), lambda *g: (tile_of(*g), 0)),
           _resident((3, D_C))]
    return ins, pl.BlockSpec((TS_SC, D_C), lambda *g: (tile_of(*g), 0))


TQ_ATT = 2048
TK_ATT = 512
NKB_ATT = SEQ // TK_ATT
LW_ATT = 256
UNROLL_ATT = 8


def _attn_kernel(q_ref, k_ref, v_ref, cp_ref, cm_ref, cn_ref, bg_ref, cw_ref, o_ref, yc_ref,
                 s0, s1, p0, p1, a0, a1, mb0, mb1, m_ref, acc_ref, ext_ref):
    nq = pl.num_programs(1)
    _shortconv_tile(pl.program_id(0) * nq + pl.program_id(1), pl.num_programs(0) * nq,
                    cp_ref, cm_ref, cn_ref, bg_ref, cw_ref, yc_ref, ext_ref)

    chunks = [slice(c, c + LW_ATT) for c in range(0, TQ_ATT, LW_ATT)]

    def scores(kb, s_ref, mb_ref, sl):
        off = pl.multiple_of(kb * TK_ATT, TK_ATT)
        st = _dot(k_ref[0, pl.ds(off, TK_ATT), :], q_ref[0, :, sl])
        s_ref[:, sl] = st
        mb_ref[:, sl] = jnp.max(st, axis=0, keepdims=True)

    def softmax(s_ref, mb_ref, p_ref, a_ref, sl):
        m_prev = m_ref[:, sl]
        m_new = jnp.maximum(m_prev, mb_ref[:, sl])
        a_ref[:, sl] = jnp.exp2(m_prev - m_new)
        m_ref[:, sl] = m_new
        p_ref[:, sl] = jnp.exp2(s_ref[:, sl] - m_new).astype(BF16)

    def values(kb, p_ref, a_ref, sl):
        acc_ref[:, sl] = a_ref[:, sl] * acc_ref[:, sl] + _dot(v_ref[0, kb], p_ref[:, sl])

    def stage(kb_scores, s_w, mb_w, s_r, mb_r, p_w, a_w, kb_values, p_r, a_r):
        for sl in chunks:
            scores(kb_scores, s_w, mb_w, sl)
            softmax(s_r, mb_r, p_w, a_w, sl)
            values(kb_values, p_r, a_r, sl)

    m_ref[...] = jnp.full(m_ref.shape, -jnp.inf, F32)
    acc_ref[...] = jnp.zeros(acc_ref.shape, F32)
    p1[...] = jnp.zeros(p1.shape, BF16)
    a1[...] = jnp.ones(a1.shape, F32)
    for sl in chunks:
        scores(0, s0, mb0, sl)

    def body(i, carry):
        for u in range(0, UNROLL_ATT, 2):
            j = UNROLL_ATT * i + u
            stage(j + 1, s1, mb1, s0, mb0, p0, a0, jnp.maximum(j - 1, 0), p1, a1)
            stage(jnp.minimum(j + 2, NKB_ATT - 1), s0, mb0, s1, mb1, p1, a1, j, p0, a0)
        return carry

    lax.fori_loop(0, NKB_ATT // UNROLL_ATT, body, 0)
    for sl in chunks:
        values(NKB_ATT - 1, p1, a1, sl)
    o_ref[...] = (acc_ref[0:V_HEAD, :] / acc_ref[V_HEAD:V_HEAD + 1, :]).T.astype(BF16)


def _attention(qt, k, vt, bg, ch, conv_w):
    nq = SEQ // TQ_ATT
    assert MLA_HEADS * nq == SEQ // TS_SC
    sc_in, sc_out = _shortconv_specs(lambda h, i: h * nq + i)
    return pl.pallas_call(
        _attn_kernel,
        grid=(MLA_HEADS, nq),
        in_specs=[pl.BlockSpec((1, QK_PAD, TQ_ATT), lambda h, i: (h, 0, i)),
                  pl.BlockSpec((1, SEQ, QK_PAD), lambda h, i: (h, 0, 0)),
                  pl.BlockSpec((1, SEQ // TK_ATT, V_ROWS, TK_ATT), lambda h, i: (h, 0, 0, 0))] + sc_in,
        out_specs=[pl.BlockSpec((TQ_ATT, V_HEAD), lambda h, i: (i, h)), sc_out],
        out_shape=[jax.ShapeDtypeStruct((SEQ, D_ATT), BF16), jax.ShapeDtypeStruct((SEQ, D_C), BF16)],
        scratch_shapes=[pltpu.VMEM((TK_ATT, TQ_ATT), F32), pltpu.VMEM((TK_ATT, TQ_ATT), F32),
                        pltpu.VMEM((TK_ATT, TQ_ATT), BF16), pltpu.VMEM((TK_ATT, TQ_ATT), BF16),
                        pltpu.VMEM((1, TQ_ATT), F32), pltpu.VMEM((1, TQ_ATT), F32),
                        pltpu.VMEM((1, TQ_ATT), F32), pltpu.VMEM((1, TQ_ATT), F32),
                        pltpu.VMEM((1, TQ_ATT), F32),
                        pltpu.VMEM((V_ROWS, TQ_ATT), F32),
                        pltpu.VMEM((TS_SC + 2 * HALO_SC, D_C), F32)],
        compiler_params=_params(("arbitrary", "arbitrary")),
        name="mla_attention",
    )(qt, k, vt, ch, ch, ch, bg, conv_w)


def _rope_tiles(w_rope):
    half = QK_ROPE // 2
    z = jnp.zeros((w_rope.shape[0], LANES - QK_ROPE), w_rope.dtype)
    a = jnp.concatenate([w_rope, z], axis=1)
    b = jnp.concatenate([w_rope[:, half:], w_rope[:, :half], z], axis=1)
    return a, b


def _rope_tables():
    inv = 1.0 / (ROPE_THETA ** (jnp.arange(0, QK_ROPE, 2, dtype=F32) / QK_ROPE))
    ang = jnp.arange(SEQ, dtype=F32)[:, None] * inv[None, :]
    cos, sin = jnp.cos(ang), jnp.sin(ang)
    z = jnp.zeros((SEQ, LANES - QK_ROPE), F32)
    return (jnp.concatenate([cos, cos, z], axis=1), jnp.concatenate([-sin, sin, z], axis=1))


def kernel(x, mix_norm_e, w_in_e, conv_a_w, conv_a_b, ln_a_g, ln_a_b, w_out_e, mix_norm_o, w_in_o,
           conv_c_w, q_norm_g, w_uq, kv_norm_g, w_ukv, w_out_o, mlp_norm, w_up, w_down, final_norm):
    xs = x[0]
    row = lambda v: v.reshape(1, -1)

    hglu, uf = _inproj_e(xs, row(mix_norm_e[0]), w_in_e[0].astype(BF16))
    ya = _conformer(hglu, conv_a_w[0], row(conv_a_b[0]), row(ln_a_g[0]), row(ln_a_b[0]))
    yb = _fourier_mix(uf)
    w_up_b = w_up.astype(BF16)
    w_down_b = w_down.astype(BF16)
    xs = _outproj_mlp(xs, ya, yb, w_out_e[0].astype(BF16), row(mlp_norm[0]),
                      w_up_b, w_down_b, row(final_norm), layer=0, final=False)

    wi = w_in_o[0]
    o = 3 * D_C + Q_LORA + KV_LORA
    kra, krb = _rope_tiles(wi[:, o:])
    w_in = jnp.concatenate([wi[:, :o], kra, krb], axis=1).astype(BF16)
    wq = w_uq[0].reshape(Q_LORA, MLA_HEADS, QK_HEAD)
    qa, qb = _rope_tiles(wq[:, :, QK_NOPE:].reshape(Q_LORA * MLA_HEADS, QK_ROPE))
    wuq = jnp.concatenate([wq[:, :, :QK_NOPE], qa.reshape(Q_LORA, MLA_HEADS, LANES),
                           qb.reshape(Q_LORA, MLA_HEADS, LANES)], axis=2)
    wuq = wuq.reshape(Q_LORA, MLA_HEADS * Q_COLS).astype(BF16)
    cos_t, sin_t = _rope_tables()
    qscale = (QK_HEAD ** -0.5) * math.log2(math.e)
    bg, ch, q, k, v = _inproj_o(xs, row(mix_norm_o[0]), w_in,
                                row(q_norm_g[0]), wuq, row(kv_norm_g[0]), w_ukv[0].astype(BF16),
                                cos_t, sin_t, qscale)
    yd, yc = _attention(q, k, v, bg, ch, conv_c_w[0])
    xs = _outproj_mlp(xs, yc, yd, w_out_o[0].astype(BF16), row(mlp_norm[1]),
                      w_up_b, w_down_b, row(final_norm), layer=1, final=True)
    return xs[None]
```

```python
import functools
import math

import numpy as np
import jax
import jax.numpy as jnp
from jax import lax
from jax.experimental import pallas as pl
from jax.experimental.pallas import tpu as pltpu

D_MODEL = 2048
SEQ = 16384
D_A = 1024
CONF_KERNEL = 31
D_B = 1024
FNET_GROUPS = 8
FNET_GROUP_CH = 128
D_C = 1024
MLA_HEADS = 8
Q_LORA = 512
KV_LORA = 256
QK_NOPE = 128
QK_ROPE = 64
V_HEAD = 128
QK_HEAD = QK_NOPE + QK_ROPE
D_ATT = MLA_HEADS * V_HEAD
ROPE_THETA = 10000.0
D_FF = 4 * D_MODEL
EPS = 1e-6

LANES = 128
QK_PAD = 2 * LANES
V_ROWS = V_HEAD + 16
FFT_N1 = 128
FFT_N2 = 128
VMEM_LIMIT = 56 * 1024 * 1024

BF16 = jnp.bfloat16
F32 = jnp.float32


def _dot(a, b):
    return jnp.dot(a, b, preferred_element_type=F32)


def _rms(x, g):
    return x * lax.rsqrt(jnp.mean(x * x, axis=-1, keepdims=True) + EPS) * g


def _params(sem):
    return pltpu.CompilerParams(dimension_semantics=sem, vmem_limit_bytes=VMEM_LIMIT)


def _resident(shape):
    nd = len(shape)
    return pl.BlockSpec(shape, lambda *_: (0,) * nd, pipeline_mode=pl.Buffered(1))


TM_IN = 512


def _inproj_e_kernel(x_ref, g_ref, w_ref, glu_ref, uf_ref):
    h = _rms(x_ref[...], g_ref[...]).astype(BF16)
    for c in range(0, D_A, 512):
        val = _dot(h, w_ref[:, c:c + 512])
        gate = _dot(h, w_ref[:, D_A + c:D_A + c + 512])
        glu_ref[:, c:c + 512] = val * jax.nn.sigmoid(gate)
    for c in range(0, D_B, 512):
        uf_ref[:, c:c + 512] = _dot(h, w_ref[:, 2 * D_A + c:2 * D_A + c + 512]).astype(BF16)


def _inproj_e(x, g, w):
    return pl.pallas_call(
        _inproj_e_kernel,
        grid=(SEQ // TM_IN,),
        in_specs=[pl.BlockSpec((TM_IN, D_MODEL), lambda i: (i, 0)),
                  _resident((1, D_MODEL)),
                  _resident((D_MODEL, 2 * D_A + D_B))],
        out_specs=[pl.BlockSpec((TM_IN, D_A), lambda i: (i, 0)),
                   pl.BlockSpec((TM_IN, D_B), lambda i: (i, 0))],
        out_shape=[jax.ShapeDtypeStruct((SEQ, D_A), F32),
                   jax.ShapeDtypeStruct((SEQ, D_B), BF16)],
        compiler_params=_params(("arbitrary",)),
        name="inproj_e",
    )(x, g, w)


TC_CONF = 128
HALO_CONF = 16


SUBLANES = 8
CONF_Q = (-2, -1, 0, 1, 2)


def _conf_tap_plan():
    plan = np.full((len(CONF_Q), SUBLANES, SUBLANES), -1, np.int32)
    for qi, q in enumerate(CONF_Q):
        for r in range(SUBLANES):
            for i in range(SUBLANES):
                d = 8 * q + r if i + r < SUBLANES else 8 * (q - 1) + r
                if abs(d) <= CONF_KERNEL // 2:
                    plan[qi, r, i] = d + CONF_KERNEL // 2
    return plan


def _conformer_kernel(prev_ref, main_ref, next_ref, w_ref, b_ref, lg_ref, lb_ref, o_ref,
                      hext_ref, conv_ref, *, live):
    i = pl.program_id(0)
    n = pl.num_programs(0)
    hext_ref[0:HALO_CONF, :] = jnp.where(i > 0, prev_ref[...], 0.0)
    hext_ref[HALO_CONF:HALO_CONF + TC_CONF, :] = main_ref[...]
    hext_ref[HALO_CONF + TC_CONF:, :] = jnp.where(i < n - 1, next_ref[...], 0.0)
    nt_out = TC_CONF // SUBLANES
    nt_ext = (TC_CONF + 2 * HALO_CONF) // SUBLANES
    t0 = HALO_CONF // SUBLANES
    for c in range(0, D_A, LANES):
        x3 = hext_ref[:, c:c + LANES].reshape(nt_ext, SUBLANES, LANES)
        acc = jnp.zeros((nt_out, SUBLANES, LANES), F32) + b_ref[:, c:c + LANES]
        for r in range(SUBLANES):
            xr = x3 if r == 0 else pltpu.roll(x3, SUBLANES - r, axis=1)
            for qi, q in enumerate(CONF_Q):
                if live[qi][r]:
                    row = (qi * SUBLANES + r) * SUBLANES
                    acc = acc + w_ref[row:row + SUBLANES, c:c + LANES] * xr[t0 + q:t0 + q + nt_out]
        conv_ref[:, c:c + LANES] = acc.reshape(TC_CONF, LANES)
    y = conv_ref[...]
    mu = jnp.mean(y, axis=-1, keepdims=True)
    d = y - mu
    var = jnp.mean(d * d, axis=-1, keepdims=True)
    z = d * lax.rsqrt(var + EPS) * lg_ref[...] + lb_ref[...]
    o_ref[...] = (z * jax.nn.sigmoid(z)).astype(BF16)


def _conformer(hglu, w, b, lg, lb):
    r = TC_CONF // HALO_CONF
    nh = SEQ // HALO_CONF
    plan = _conf_tap_plan()
    live = tuple(tuple(bool((plan[qi, rr] >= 0).any()) for rr in range(SUBLANES))
                 for qi in range(len(CONF_Q)))
    wtab = jnp.where((plan >= 0)[..., None], w[np.maximum(plan, 0)], 0.0).reshape(-1, D_A)
    return pl.pallas_call(
        functools.partial(_conformer_kernel, live=live),
        grid=(SEQ // TC_CONF,),
        in_specs=[pl.BlockSpec((HALO_CONF, D_A), lambda i: (jnp.maximum(i * r - 1, 0), 0)),
                  pl.BlockSpec((TC_CONF, D_A), lambda i: (i, 0)),
                  pl.BlockSpec((HALO_CONF, D_A), lambda i: (jnp.minimum((i + 1) * r, nh - 1), 0)),
                  _resident((len(CONF_Q) * SUBLANES * SUBLANES, D_A)),
                  _resident((1, D_A)), _resident((1, D_A)), _resident((1, D_A))],
        out_specs=pl.BlockSpec((TC_CONF, D_A), lambda i: (i, 0)),
        out_shape=jax.ShapeDtypeStruct((SEQ, D_A), BF16),
        scratch_shapes=[pltpu.VMEM((TC_CONF + 2 * HALO_CONF, D_A), F32),
                        pltpu.VMEM((TC_CONF, D_A), F32)],
        compiler_params=_params(("arbitrary",)),
        name="conformer",
    )(hglu, hglu, hglu, wtab, b, lg, lb)


TN_FFT1 = 8192
TK1_FFT2 = 8


def _fft1_kernel(x_ref, f_ref, tr_ref, ti_ref):
    for c in range(0, TN_FFT1, 1024):
        t = _dot(f_ref[...], x_ref[:, c:c + 1024])
        tr_ref[:, c:c + 1024] = t[:FFT_N1].astype(BF16)
        ti_ref[:, c:c + 1024] = t[FFT_N1:].astype(BF16)


def _fft1(x2d, f_stack):
    ncol = FFT_N2 * D_B
    return pl.pallas_call(
        _fft1_kernel,
        grid=(ncol // TN_FFT1,),
        in_specs=[pl.BlockSpec((FFT_N1, TN_FFT1), lambda j: (0, j)),
                  _resident((2 * FFT_N1, FFT_N1))],
        out_specs=[pl.BlockSpec((FFT_N1, TN_FFT1), lambda j: (0, j)),
                   pl.BlockSpec((FFT_N1, TN_FFT1), lambda j: (0, j))],
        out_shape=[jax.ShapeDtypeStruct((FFT_N1, ncol), BF16),
                   jax.ShapeDtypeStruct((FFT_N1, ncol), BF16)],
        compiler_params=_params(("arbitrary",)),
        name="fft_stage1",
    )(x2d, f_stack)


def _fft2_kernel(tr_ref, ti_ref, m_ref, cs_ref, o_ref):
    for j in range(TK1_FFT2):
        t = jnp.concatenate([tr_ref[j], ti_ref[j]], axis=0)
        z = _dot(m_ref[j], t)
        zr = z[:FFT_N2].astype(BF16)
        zi = z[FFT_N2:].astype(BF16)
        for g in range(FNET_GROUPS):
            lo = g * FNET_GROUP_CH
            zz = jnp.concatenate([zr[:, lo:lo + FNET_GROUP_CH], zi[:, lo:lo + FNET_GROUP_CH]], axis=1)
            o_ref[:, j * D_B + lo:j * D_B + lo + FNET_GROUP_CH] = _dot(zz, cs_ref[...]).astype(BF16)


def _fft2(tr3, ti3, m_full, cs):
    return pl.pallas_call(
        _fft2_kernel,
        grid=(FFT_N1 // TK1_FFT2,),
        in_specs=[pl.BlockSpec((TK1_FFT2, FFT_N2, D_B), lambda i: (i, 0, 0)),
                  pl.BlockSpec((TK1_FFT2, FFT_N2, D_B), lambda i: (i, 0, 0)),
                  pl.BlockSpec((TK1_FFT2, 2 * FFT_N2, 2 * FFT_N2), lambda i: (i, 0, 0)),
                  _resident((2 * FNET_GROUP_CH, FNET_GROUP_CH))],
        out_specs=pl.BlockSpec((FFT_N2, TK1_FFT2 * D_B), lambda i: (0, i)),
        out_shape=jax.ShapeDtypeStruct((FFT_N2, FFT_N1 * D_B), BF16),
        compiler_params=_params(("arbitrary",)),
        name="fft_stage2",
    )(tr3, ti3, m_full, cs)


def _dft_tables():
    n = np.arange(128)
    ph = 2.0 * np.pi * ((n[:, None] * n[None, :]) % 128) / 128.0
    fr, fi = np.cos(ph), -np.sin(ph)
    f_stack = np.concatenate([fr, fi], axis=0) / math.sqrt(SEQ)
    tw_ph = 2.0 * np.pi * (n[:, None] * n[None, :]) / SEQ
    twr, twi = np.cos(tw_ph), -np.sin(tw_ph)
    cs = np.concatenate([np.cos(ph), np.sin(ph)], axis=0) / math.sqrt(FNET_GROUP_CH)
    return (jnp.asarray(f_stack, F32).astype(BF16), jnp.asarray(fr, F32), jnp.asarray(fi, F32),
            jnp.asarray(twr, F32), jnp.asarray(twi, F32), jnp.asarray(cs, F32).astype(BF16))


def _fourier_mix(uf):
    f_stack, fr, fi, twr, twi, cs = _dft_tables()
    mr = fr[None] * twr[:, None, :] - fi[None] * twi[:, None, :]
    mi = fr[None] * twi[:, None, :] + fi[None] * twr[:, None, :]
    m_full = jnp.concatenate([jnp.concatenate([mr, -mi], axis=2),
                              jnp.concatenate([mi, mr], axis=2)], axis=1).astype(BF16)
    tr, ti = _fft1(uf.reshape(FFT_N1, FFT_N2 * D_B), f_stack)
    y2d = _fft2(tr.reshape(FFT_N1, FFT_N2, D_B), ti.reshape(FFT_N1, FFT_N2, D_B), m_full, cs)
    return y2d.reshape(SEQ, D_B)


TM_MLP = 512
TF_MLP = 1024


def _mlp_kernel(x_ref, ya_ref, yb_ref, wo_ref, g_ref, wup_ref, wdn_ref, gf_ref, o_ref, h_ref, *,
                final):
    f = pl.program_id(1)
    half = wo_ref.shape[0] // 2

    @pl.when(f == 0)
    def _():
        for c in range(0, D_MODEL, 512):
            o_ref[:, c:c + 512] = (x_ref[:, c:c + 512]
                                   + _dot(ya_ref[...], wo_ref[0:half, c:c + 512])
                                   + _dot(yb_ref[...], wo_ref[half:, c:c + 512]))
        h_ref[...] = _rms(o_ref[...], g_ref[...]).astype(BF16)

    a = _dot(h_ref[...], wup_ref[...])
    a = jnp.square(jnp.maximum(a, 0.0)).astype(BF16)
    for c in range(0, D_MODEL, 512):
        o_ref[:, c:c + 512] += _dot(a, wdn_ref[:, c:c + 512])

    if final:
        @pl.when(f == pl.num_programs(1) - 1)
        def _():
            o_ref[...] = _rms(o_ref[...], gf_ref[...])


def _outproj_mlp(x, ya, yb, wo, g, wup, wdn, gf, layer, final):
    return pl.pallas_call(
        functools.partial(_mlp_kernel, final=final),
        grid=(SEQ // TM_MLP, D_FF // TF_MLP),
        in_specs=[pl.BlockSpec((TM_MLP, D_MODEL), lambda i, f: (i, 0)),
                  pl.BlockSpec((TM_MLP, ya.shape[1]), lambda i, f: (i, 0)),
                  pl.BlockSpec((TM_MLP, yb.shape[1]), lambda i, f: (i, 0)),
                  _resident(wo.shape),
                  _resident((1, D_MODEL)),
                  pl.BlockSpec((None, D_MODEL, TF_MLP), lambda i, f: (layer, 0, f)),
                  pl.BlockSpec((None, TF_MLP, D_MODEL), lambda i, f: (layer, f, 0)),
                  _resident((1, D_MODEL))],
        out_specs=pl.BlockSpec((TM_MLP, D_MODEL), lambda i, f: (i, 0)),
        out_shape=jax.ShapeDtypeStruct((SEQ, D_MODEL), F32),
        scratch_shapes=[pltpu.VMEM((TM_MLP, D_MODEL), BF16)],
        compiler_params=_params(("arbitrary", "arbitrary")),
        name="outproj_mlp_final" if final else "outproj_mlp",
    )(x, ya, yb, wo, g, wup, wdn, gf)


TM_INO = 256
Q_COLS = 3 * LANES


def _inproj_o_kernel(x_ref, g_ref, w_ref, qg_ref, wuq_ref, kvg_ref, wukv_ref, cos_ref, sin_ref,
                     bg_ref, ch_ref, q_ref, k_ref, v_ref, *, qscale):
    h = _rms(x_ref[...], g_ref[...]).astype(BF16)
    for c in range(0, D_C, 512):
        bg_ref[:, c:c + 512] = _dot(h, w_ref[:, c:c + 512])
        ch_ref[:, c:c + 512] = (_dot(h, w_ref[:, D_C + c:D_C + c + 512])
                                * _dot(h, w_ref[:, 2 * D_C + c:2 * D_C + c + 512]))
    cosv = cos_ref[...]
    sinv = sin_ref[...]
    tail = lax.broadcasted_iota(jnp.int32, (V_ROWS - V_HEAD, x_ref.shape[0]), 0)
    ones_row = jnp.where(tail == 0, 1.0, 0.0).astype(BF16)
    lat = _dot(h, w_ref[:, 3 * D_C:])
    o = Q_LORA + KV_LORA
    kr = (lat[:, o:o + LANES] * cosv + lat[:, o + LANES:o + 2 * LANES] * sinv).astype(BF16)
    cq = _rms(lat[:, :Q_LORA], qg_ref[...]).astype(BF16)
    ckv = _rms(lat[:, Q_LORA:o], kvg_ref[...]).astype(BF16)
    for hd in range(MLA_HEADS):
        qh = _dot(cq, wuq_ref[:, hd * Q_COLS:(hd + 1) * Q_COLS])
        q_ref[hd, 0:LANES, :] = (qh[:, 0:LANES] * qscale).T.astype(BF16)
        q_ref[hd, LANES:, :] = ((qh[:, LANES:2 * LANES] * cosv + qh[:, 2 * LANES:] * sinv)
                                * qscale).T.astype(BF16)
        kv = _dot(ckv, wukv_ref[:, hd * 2 * LANES:(hd + 1) * 2 * LANES])
        k_ref[hd, :, 0:LANES] = kv[:, 0:LANES].astype(BF16)
        k_ref[hd, :, LANES:] = kr
        v_ref[hd, 0, 0:V_HEAD, :] = kv[:, LANES:].T.astype(BF16)
        v_ref[hd, 0, V_HEAD:, :] = ones_row


def _inproj_o(x, g, w, qg, wuq, kvg, wukv, cos_t, sin_t, qscale):
    tm = TM_INO
    row = lambda i: (i, 0)
    head = lambda i: (0, i, 0)
    return pl.pallas_call(
        functools.partial(_inproj_o_kernel, qscale=qscale),
        grid=(SEQ // tm,),
        in_specs=[pl.BlockSpec((tm, D_MODEL), row),
                  _resident((1, D_MODEL)),
                  _resident(w.shape),
                  _resident((1, Q_LORA)), _resident(wuq.shape),
                  _resident((1, KV_LORA)), _resident(wukv.shape),
                  pl.BlockSpec((tm, LANES), row), pl.BlockSpec((tm, LANES), row)],
        out_specs=[pl.BlockSpec((tm, D_C), row), pl.BlockSpec((tm, D_C), row),
                   pl.BlockSpec((MLA_HEADS, QK_PAD, tm), lambda i: (0, 0, i)),
                   pl.BlockSpec((MLA_HEADS, tm, QK_PAD), head),
                   pl.BlockSpec((MLA_HEADS, 1, V_ROWS, tm),
                                lambda i: (0, i // (TK_ATT // tm), 0, i % (TK_ATT // tm)))],
        out_shape=[jax.ShapeDtypeStruct((SEQ, D_C), F32), jax.ShapeDtypeStruct((SEQ, D_C), F32),
                   jax.ShapeDtypeStruct((MLA_HEADS, QK_PAD, SEQ), BF16),
                   jax.ShapeDtypeStruct((MLA_HEADS, SEQ, QK_PAD), BF16),
                   jax.ShapeDtypeStruct((MLA_HEADS, SEQ // TK_ATT, V_ROWS, TK_ATT), BF16)],
        compiler_params=_params(("arbitrary",)),
        name="inproj_o",
    )(x, g, w, qg, wuq, kvg, wukv, cos_t, sin_t)


TS_SC = 256
HALO_SC = 8


def _shortconv_tile(t, n, prev_ref, main_ref, next_ref, bg_ref, w_ref, o_ref, ext_ref):
    ext_ref[0:HALO_SC, :] = jnp.where(t > 0, prev_ref[...], 0.0)
    ext_ref[HALO_SC:HALO_SC + TS_SC, :] = main_ref[...]
    ext_ref[HALO_SC + TS_SC:, :] = jnp.where(t < n - 1, next_ref[...], 0.0)
    for r0 in range(0, TS_SC, 128):
        for c in range(0, D_C, LANES):
            cs = slice(c, c + LANES)
            acc = w_ref[0:1, cs] * ext_ref[HALO_SC - 1 + r0:HALO_SC - 1 + r0 + 128, cs]
            acc = acc + w_ref[1:2, cs] * ext_ref[HALO_SC + r0:HALO_SC + r0 + 128, cs]
            acc = acc + w_ref[2:3, cs] * ext_ref[HALO_SC + 1 + r0:HALO_SC + 1 + r0 + 128, cs]
            o_ref[r0:r0 + 128, cs] = (bg_ref[r0:r0 + 128, cs] * acc).astype(BF16)


def _shortconv_specs(tile_of):
    r = TS_SC // HALO_SC
    nh = SEQ // HALO_SC
    ins = [pl.BlockSpec((HALO_SC, D_C), lambda *g: (jnp.maximum(tile_of(*g) * r - 1, 0), 0)),
           pl.BlockSpec((TS_SC, D_C), lambda *g: (tile_of(*g), 0)),
           pl.BlockSpec((HALO_SC, D_C), lambda *g: (jnp.minimum((tile_of(*g) + 1) * r, nh - 1), 0)),
           pl.BlockSpec((TS_SC, D_C), lambda *g: (tile_of(*g), 0)),
           _resident((3, D_C))]
    return ins, pl.BlockSpec((TS_SC, D_C), lambda *g: (tile_of(*g), 0))


TQ_ATT = 2048
TK_ATT = 512
NKB_ATT = SEQ // TK_ATT
LW_ATT = 256
UNROLL_ATT = 8


def _attn_kernel(q_ref, k_ref, v_ref, cp_ref, cm_ref, cn_ref, bg_ref, cw_ref, o_ref, yc_ref,
                 s0, s1, p0, p1, a0, a1, mb0, mb1, m_ref, acc_ref, ext_ref):
    nq = pl.num_programs(1)
    _shortconv_tile(pl.program_id(0) * nq + pl.program_id(1), pl.num_programs(0) * nq,
                    cp_ref, cm_ref, cn_ref, bg_ref, cw_ref, yc_ref, ext_ref)

    chunks = [slice(c, c + LW_ATT) for c in range(0, TQ_ATT, LW_ATT)]

    def scores(kb, s_ref, mb_ref, sl):
        off = pl.multiple_of(kb * TK_ATT, TK_ATT)
        st = _dot(k_ref[0, pl.ds(off, TK_ATT), :], q_ref[0, :, sl])
        s_ref[:, sl] = st
        mb_ref[:, sl] = jnp.max(st, axis=0, keepdims=True)

    def softmax(s_ref, mb_ref, p_ref, a_ref, sl):
        m_prev = m_ref[:, sl]
        m_new = jnp.maximum(m_prev, mb_ref[:, sl])
        a_ref[:, sl] = jnp.exp2(m_prev - m_new)
        m_ref[:, sl] = m_new
        p_ref[:, sl] = jnp.exp2(s_ref[:, sl] - m_new).astype(BF16)

    def values(kb, p_ref, a_ref, sl):
        acc_ref[:, sl] = a_ref[:, sl] * acc_ref[:, sl] + _dot(v_ref[0, kb], p_ref[:, sl])

    def stage(kb_scores, s_w, mb_w, s_r, mb_r, p_w, a_w, kb_values, p_r, a_r):
        for sl in chunks:
            scores(kb_scores, s_w, mb_w, sl)
            softmax(s_r, mb_r, p_w, a_w, sl)
            values(kb_values, p_r, a_r, sl)

    m_ref[...] = jnp.full(m_ref.shape, -jnp.inf, F32)
    acc_ref[...] = jnp.zeros(acc_ref.shape, F32)
    p1[...] = jnp.zeros(p1.shape, BF16)
    a1[...] = jnp.ones(a1.shape, F32)
    for sl in chunks:
        scores(0, s0, mb0, sl)

    def body(i, carry):
        for u in range(0, UNROLL_ATT, 2):
            j = UNROLL_ATT * i + u
            stage(j + 1, s1, mb1, s0, mb0, p0, a0, jnp.maximum(j - 1, 0), p1, a1)
            stage(jnp.minimum(j + 2, NKB_ATT - 1), s0, mb0, s1, mb1, p1, a1, j, p0, a0)
        return carry

    lax.fori_loop(0, NKB_ATT // UNROLL_ATT, body, 0)
    for sl in chunks:
        values(NKB_ATT - 1, p1, a1, sl)
    o_ref[...] = (acc_ref[0:V_HEAD, :] / acc_ref[V_HEAD:V_HEAD + 1, :]).T.astype(BF16)


def _attention(qt, k, vt, bg, ch, conv_w):
    nq = SEQ // TQ_ATT
    assert MLA_HEADS * nq == SEQ // TS_SC
    sc_in, sc_out = _shortconv_specs(lambda h, i: h * nq + i)
    return pl.pallas_call(
        _attn_kernel,
        grid=(MLA_HEADS, nq),
        in_specs=[pl.BlockSpec((1, QK_PAD, TQ_ATT), lambda h, i: (h, 0, i)),
                  pl.BlockSpec((1, SEQ, QK_PAD), lambda h, i: (h, 0, 0)),
                  pl.BlockSpec((1, SEQ // TK_ATT, V_ROWS, TK_ATT), lambda h, i: (h, 0, 0, 0))] + sc_in,
        out_specs=[pl.BlockSpec((TQ_ATT, V_HEAD), lambda h, i: (i, h)), sc_out],
        out_shape=[jax.ShapeDtypeStruct((SEQ, D_ATT), BF16), jax.ShapeDtypeStruct((SEQ, D_C), BF16)],
        scratch_shapes=[pltpu.VMEM((TK_ATT, TQ_ATT), F32), pltpu.VMEM((TK_ATT, TQ_ATT), F32),
                        pltpu.VMEM((TK_ATT, TQ_ATT), BF16), pltpu.VMEM((TK_ATT, TQ_ATT), BF16),
                        pltpu.VMEM((1, TQ_ATT), F32), pltpu.VMEM((1, TQ_ATT), F32),
                        pltpu.VMEM((1, TQ_ATT), F32), pltpu.VMEM((1, TQ_ATT), F32),
                        pltpu.VMEM((1, TQ_ATT), F32),
                        pltpu.VMEM((V_ROWS, TQ_ATT), F32),
                        pltpu.VMEM((TS_SC + 2 * HALO_SC, D_C), F32)],
        compiler_params=_params(("arbitrary", "arbitrary")),
        name="mla_attention",
    )(qt, k, vt, ch, ch, ch, bg, conv_w)


def _rope_tiles(w_rope):
    half = QK_ROPE // 2
    z = jnp.zeros((w_rope.shape[0], LANES - QK_ROPE), w_rope.dtype)
    a = jnp.concatenate([w_rope, z], axis=1)
    b = jnp.concatenate([w_rope[:, half:], w_rope[:, :half], z], axis=1)
    return a, b


def _rope_tables():
    inv = 1.0 / (ROPE_THETA ** (jnp.arange(0, QK_ROPE, 2, dtype=F32) / QK_ROPE))
    hi = (float(LANES) * jnp.arange(SEQ // LANES, dtype=F32))[:, None] * inv[None, :]
    lo = jnp.arange(LANES, dtype=F32)[:, None] * inv[None, :]
    ch, sh, cl, sl = jnp.cos(hi)[:, None], jnp.sin(hi)[:, None], jnp.cos(lo)[None], jnp.sin(lo)[None]
    cos = (ch * cl - sh * sl).reshape(SEQ, QK_ROPE // 2)
    sin = (sh * cl + ch * sl).reshape(SEQ, QK_ROPE // 2)
    z = jnp.zeros((SEQ, LANES - QK_ROPE), F32)
    return (jnp.concatenate([cos, cos, z], axis=1), jnp.concatenate([-sin, sin, z], axis=1))


def kernel(x, mix_norm_e, w_in_e, conv_a_w, conv_a_b, ln_a_g, ln_a_b, w_out_e, mix_norm_o, w_in_o,
           conv_c_w, q_norm_g, w_uq, kv_norm_g, w_ukv, w_out_o, mlp_norm, w_up, w_down, final_norm):
    xs = x[0]
    row = lambda v: v.reshape(1, -1)

    hglu, uf = _inproj_e(xs, row(mix_norm_e[0]), w_in_e[0].astype(BF16))
    ya = _conformer(hglu, conv_a_w[0], row(conv_a_b[0]), row(ln_a_g[0]), row(ln_a_b[0]))
    yb = _fourier_mix(uf)
    w_up_b = w_up.astype(BF16)
    w_down_b = w_down.astype(BF16)
    xs = _outproj_mlp(xs, ya, yb, w_out_e[0].astype(BF16), row(mlp_norm[0]),
                      w_up_b, w_down_b, row(final_norm), layer=0, final=False)

    wi = w_in_o[0]
    o = 3 * D_C + Q_LORA + KV_LORA
    kra, krb = _rope_tiles(wi[:, o:])
    w_in = jnp.concatenate([wi[:, :o], kra, krb], axis=1).astype(BF16)
    wq = w_uq[0].reshape(Q_LORA, MLA_HEADS, QK_HEAD)
    qa, qb = _rope_tiles(wq[:, :, QK_NOPE:].reshape(Q_LORA * MLA_HEADS, QK_ROPE))
    wuq = jnp.concatenate([wq[:, :, :QK_NOPE], qa.reshape(Q_LORA, MLA_HEADS, LANES),
                           qb.reshape(Q_LORA, MLA_HEADS, LANES)], axis=2)
    wuq = wuq.reshape(Q_LORA, MLA_HEADS * Q_COLS).astype(BF16)
    cos_t, sin_t = _rope_tables()
    qscale = (QK_HEAD ** -0.5) * math.log2(math.e)
    bg, ch, q, k, v = _inproj_o(xs, row(mix_norm_o[0]), w_in,
                                row(q_norm_g[0]), wuq, row(kv_norm_g[0]), w_ukv[0].astype(BF16),
                                cos_t, sin_t, qscale)
    yd, yc = _attention(q, k, v, bg, ch, conv_c_w[0])
    xs = _outproj_mlp(xs, yc, yd, w_out_o[0].astype(BF16), row(mlp_norm[1]),
                      w_up_b, w_down_b, row(final_norm), layer=1, final=True)
    return xs[None]
```

```python
import functools
import math

import numpy as np
import jax
import jax.numpy as jnp
from jax import lax
from jax.experimental import pallas as pl
from jax.experimental.pallas import tpu as pltpu

D_MODEL = 2048
SEQ = 16384
D_A = 1024
CONF_KERNEL = 31
D_B = 1024
FNET_GROUPS = 8
FNET_GROUP_CH = 128
D_C = 1024
MLA_HEADS = 8
Q_LORA = 512
KV_LORA = 256
QK_NOPE = 128
QK_ROPE = 64
V_HEAD = 128
QK_HEAD = QK_NOPE + QK_ROPE
D_ATT = MLA_HEADS * V_HEAD
ROPE_THETA = 10000.0
D_FF = 4 * D_MODEL
EPS = 1e-6

LANES = 128
QK_PAD = 2 * LANES
V_ROWS = V_HEAD + 16
FFT_N1 = 128
FFT_N2 = 128
VMEM_LIMIT = 56 * 1024 * 1024

BF16 = jnp.bfloat16
F32 = jnp.float32


def _dot(a, b):
    return jnp.dot(a, b, preferred_element_type=F32)


def _rms(x, g):
    return x * lax.rsqrt(jnp.mean(x * x, axis=-1, keepdims=True) + EPS) * g


def _params(sem):
    return pltpu.CompilerParams(dimension_semantics=sem, vmem_limit_bytes=VMEM_LIMIT)


def _resident(shape):
    nd = len(shape)
    return pl.BlockSpec(shape, lambda *_: (0,) * nd, pipeline_mode=pl.Buffered(1))


TM_IN = 512


def _inproj_e_kernel(x_ref, g_ref, w_ref, glu_ref, uf_ref):
    h = _rms(x_ref[...], g_ref[...]).astype(BF16)
    for c in range(0, D_A, 512):
        val = _dot(h, w_ref[:, c:c + 512])
        gate = _dot(h, w_ref[:, D_A + c:D_A + c + 512])
        glu_ref[:, c:c + 512] = val * jax.nn.sigmoid(gate)
    for c in range(0, D_B, 512):
        uf_ref[:, c:c + 512] = _dot(h, w_ref[:, 2 * D_A + c:2 * D_A + c + 512]).astype(BF16)


def _inproj_e(x, g, w):
    return pl.pallas_call(
        _inproj_e_kernel,
        grid=(SEQ // TM_IN,),
        in_specs=[pl.BlockSpec((TM_IN, D_MODEL), lambda i: (i, 0)),
                  _resident((1, D_MODEL)),
                  _resident((D_MODEL, 2 * D_A + D_B))],
        out_specs=[pl.BlockSpec((TM_IN, D_A), lambda i: (i, 0)),
                   pl.BlockSpec((TM_IN, D_B), lambda i: (i, 0))],
        out_shape=[jax.ShapeDtypeStruct((SEQ, D_A), F32),
                   jax.ShapeDtypeStruct((SEQ, D_B), BF16)],
        compiler_params=_params(("arbitrary",)),
        name="inproj_e",
    )(x, g, w)


TC_CONF = 128
HALO_CONF = 16


SUBLANES = 8
CONF_Q = (-2, -1, 0, 1, 2)


def _conf_tap_plan():
    plan = np.full((len(CONF_Q), SUBLANES, SUBLANES), -1, np.int32)
    for qi, q in enumerate(CONF_Q):
        for r in range(SUBLANES):
            for i in range(SUBLANES):
                d = 8 * q + r if i + r < SUBLANES else 8 * (q - 1) + r
                if abs(d) <= CONF_KERNEL // 2:
                    plan[qi, r, i] = d + CONF_KERNEL // 2
    return plan


def _conformer_kernel(prev_ref, main_ref, next_ref, w_ref, b_ref, lg_ref, lb_ref, o_ref,
                      hext_ref, conv_ref, *, live):
    i = pl.program_id(0)
    n = pl.num_programs(0)
    hext_ref[0:HALO_CONF, :] = jnp.where(i > 0, prev_ref[...], 0.0)
    hext_ref[HALO_CONF:HALO_CONF + TC_CONF, :] = main_ref[...]
    hext_ref[HALO_CONF + TC_CONF:, :] = jnp.where(i < n - 1, next_ref[...], 0.0)
    nt_out = TC_CONF // SUBLANES
    nt_ext = (TC_CONF + 2 * HALO_CONF) // SUBLANES
    t0 = HALO_CONF // SUBLANES
    for c in range(0, D_A, LANES):
        x3 = hext_ref[:, c:c + LANES].reshape(nt_ext, SUBLANES, LANES)
        acc = jnp.zeros((nt_out, SUBLANES, LANES), F32) + b_ref[:, c:c + LANES]
        for r in range(SUBLANES):
            xr = x3 if r == 0 else pltpu.roll(x3, SUBLANES - r, axis=1)
            for qi, q in enumerate(CONF_Q):
                if live[qi][r]:
                    row = (qi * SUBLANES + r) * SUBLANES
                    acc = acc + w_ref[row:row + SUBLANES, c:c + LANES] * xr[t0 + q:t0 + q + nt_out]
        conv_ref[:, c:c + LANES] = acc.reshape(TC_CONF, LANES)
    y = conv_ref[...]
    mu = jnp.mean(y, axis=-1, keepdims=True)
    d = y - mu
    var = jnp.mean(d * d, axis=-1, keepdims=True)
    z = d * lax.rsqrt(var + EPS) * lg_ref[...] + lb_ref[...]
    o_ref[...] = (z * jax.nn.sigmoid(z)).astype(BF16)


def _conformer(hglu, w, b, lg, lb):
    r = TC_CONF // HALO_CONF
    nh = SEQ // HALO_CONF
    plan = _conf_tap_plan()
    live = tuple(tuple(bool((plan[qi, rr] >= 0).any()) for rr in range(SUBLANES))
                 for qi in range(len(CONF_Q)))
    wtab = jnp.where((plan >= 0)[..., None], w[np.maximum(plan, 0)], 0.0).reshape(-1, D_A)
    return pl.pallas_call(
        functools.partial(_conformer_kernel, live=live),
        grid=(SEQ // TC_CONF,),
        in_specs=[pl.BlockSpec((HALO_CONF, D_A), lambda i: (jnp.maximum(i * r - 1, 0), 0)),
                  pl.BlockSpec((TC_CONF, D_A), lambda i: (i, 0)),
                  pl.BlockSpec((HALO_CONF, D_A), lambda i: (jnp.minimum((i + 1) * r, nh - 1), 0)),
                  _resident((len(CONF_Q) * SUBLANES * SUBLANES, D_A)),
                  _resident((1, D_A)), _resident((1, D_A)), _resident((1, D_A))],
        out_specs=pl.BlockSpec((TC_CONF, D_A), lambda i: (i, 0)),
        out_shape=jax.ShapeDtypeStruct((SEQ, D_A), BF16),
        scratch_shapes=[pltpu.VMEM((TC_CONF + 2 * HALO_CONF, D_A), F32),
                        pltpu.VMEM((TC_CONF, D_A), F32)],
        compiler_params=_params(("arbitrary",)),
        name="conformer",
    )(hglu, hglu, hglu, wtab, b, lg, lb)


TN_FFT1 = 8192
TK1_FFT2 = 8


def _fft1_kernel(x_ref, f_ref, tr_ref, ti_ref):
    for c in range(0, TN_FFT1, 1024):
        t = _dot(f_ref[...], x_ref[:, c:c + 1024])
        tr_ref[:, c:c + 1024] = t[:FFT_N1].astype(BF16)
        ti_ref[:, c:c + 1024] = t[FFT_N1:].astype(BF16)


def _fft1(x2d, f_stack):
    ncol = FFT_N2 * D_B
    return pl.pallas_call(
        _fft1_kernel,
        grid=(ncol // TN_FFT1,),
        in_specs=[pl.BlockSpec((FFT_N1, TN_FFT1), lambda j: (0, j)),
                  _resident((2 * FFT_N1, FFT_N1))],
        out_specs=[pl.BlockSpec((FFT_N1, TN_FFT1), lambda j: (0, j)),
                   pl.BlockSpec((FFT_N1, TN_FFT1), lambda j: (0, j))],
        out_shape=[jax.ShapeDtypeStruct((FFT_N1, ncol), BF16),
                   jax.ShapeDtypeStruct((FFT_N1, ncol), BF16)],
        compiler_params=_params(("arbitrary",)),
        name="fft_stage1",
    )(x2d, f_stack)


def _fft2_kernel(tr_ref, ti_ref, m_ref, cs_ref, o_ref):
    for j in range(TK1_FFT2):
        t = jnp.concatenate([tr_ref[j], ti_ref[j]], axis=0)
        z = _dot(m_ref[j], t)
        zr = z[:FFT_N2].astype(BF16)
        zi = z[FFT_N2:].astype(BF16)
        for g in range(FNET_GROUPS):
            lo = g * FNET_GROUP_CH
            zz = jnp.concatenate([zr[:, lo:lo + FNET_GROUP_CH], zi[:, lo:lo + FNET_GROUP_CH]], axis=1)
            o_ref[:, j * D_B + lo:j * D_B + lo + FNET_GROUP_CH] = _dot(zz, cs_ref[...]).astype(BF16)


def _fft2(tr3, ti3, m_full, cs):
    return pl.pallas_call(
        _fft2_kernel,
        grid=(FFT_N1 // TK1_FFT2,),
        in_specs=[pl.BlockSpec((TK1_FFT2, FFT_N2, D_B), lambda i: (i, 0, 0)),
                  pl.BlockSpec((TK1_FFT2, FFT_N2, D_B), lambda i: (i, 0, 0)),
                  pl.BlockSpec((TK1_FFT2, 2 * FFT_N2, 2 * FFT_N2), lambda i: (i, 0, 0)),
                  _resident((2 * FNET_GROUP_CH, FNET_GROUP_CH))],
        out_specs=pl.BlockSpec((FFT_N2, TK1_FFT2 * D_B), lambda i: (0, i)),
        out_shape=jax.ShapeDtypeStruct((FFT_N2, FFT_N1 * D_B), BF16),
        compiler_params=_params(("arbitrary",)),
        name="fft_stage2",
    )(tr3, ti3, m_full, cs)


def _dft_tables():
    n = np.arange(128)
    ph = 2.0 * np.pi * ((n[:, None] * n[None, :]) % 128) / 128.0
    fr, fi = np.cos(ph), -np.sin(ph)
    f_stack = np.concatenate([fr, fi], axis=0) / math.sqrt(SEQ)
    tw_ph = 2.0 * np.pi * (n[:, None] * n[None, :]) / SEQ
    twr, twi = np.cos(tw_ph), -np.sin(tw_ph)
    cs = np.concatenate([np.cos(ph), np.sin(ph)], axis=0) / math.sqrt(FNET_GROUP_CH)
    return (jnp.asarray(f_stack, F32).astype(BF16), jnp.asarray(fr, F32), jnp.asarray(fi, F32),
            jnp.asarray(twr, F32), jnp.asarray(twi, F32), jnp.asarray(cs, F32).astype(BF16))


def _fourier_mix(uf):
    f_stack, fr, fi, twr, twi, cs = _dft_tables()
    mr = fr[None] * twr[:, None, :] - fi[None] * twi[:, None, :]
    mi = fr[None] * twi[:, None, :] + fi[None] * twr[:, None, :]
    m_full = jnp.concatenate([jnp.concatenate([mr, -mi], axis=2),
                              jnp.concatenate([mi, mr], axis=2)], axis=1).astype(BF16)
    tr, ti = _fft1(uf.reshape(FFT_N1, FFT_N2 * D_B), f_stack)
    y2d = _fft2(tr.reshape(FFT_N1, FFT_N2, D_B), ti.reshape(FFT_N1, FFT_N2, D_B), m_full, cs)
    return y2d.reshape(SEQ, D_B)


TM_MLP = 512
TF_MLP = 1024


def _mlp_kernel(x_ref, ya_ref, yb_ref, wo_ref, g_ref, wup_ref, wdn_ref, gf_ref, o_ref, h_ref, *,
                final):
    f = pl.program_id(1)
    half = wo_ref.shape[0] // 2

    @pl.when(f == 0)
    def _():
        for c in range(0, D_MODEL, 512):
            o_ref[:, c:c + 512] = (x_ref[:, c:c + 512]
                                   + _dot(ya_ref[...], wo_ref[0:half, c:c + 512])
                                   + _dot(yb_ref[...], wo_ref[half:, c:c + 512]))
        h_ref[...] = _rms(o_ref[...], g_ref[...]).astype(BF16)

    a = _dot(h_ref[...], wup_ref[...])
    a = jnp.square(jnp.maximum(a, 0.0)).astype(BF16)
    for c in range(0, D_MODEL, 512):
        o_ref[:, c:c + 512] += _dot(a, wdn_ref[:, c:c + 512])

    if final:
        @pl.when(f == pl.num_programs(1) - 1)
        def _():
            o_ref[...] = _rms(o_ref[...], gf_ref[...])


def _outproj_mlp(x, ya, yb, wo, g, wup, wdn, gf, layer, final):
    return pl.pallas_call(
        functools.partial(_mlp_kernel, final=final),
        grid=(SEQ // TM_MLP, D_FF // TF_MLP),
        in_specs=[pl.BlockSpec((TM_MLP, D_MODEL), lambda i, f: (i, 0)),
                  pl.BlockSpec((TM_MLP, ya.shape[1]), lambda i, f: (i, 0)),
                  pl.BlockSpec((TM_MLP, yb.shape[1]), lambda i, f: (i, 0)),
                  _resident(wo.shape),
                  _resident((1, D_MODEL)),
                  pl.BlockSpec((None, D_MODEL, TF_MLP), lambda i, f: (layer, 0, f)),
                  pl.BlockSpec((None, TF_MLP, D_MODEL), lambda i, f: (layer, f, 0)),
                  _resident((1, D_MODEL))],
        out_specs=pl.BlockSpec((TM_MLP, D_MODEL), lambda i, f: (i, 0)),
        out_shape=jax.ShapeDtypeStruct((SEQ, D_MODEL), F32),
        scratch_shapes=[pltpu.VMEM((TM_MLP, D_MODEL), BF16)],
        compiler_params=_params(("arbitrary", "arbitrary")),
        name="outproj_mlp_final" if final else "outproj_mlp",
    )(x, ya, yb, wo, g, wup, wdn, gf)


TM_INO = 512
Q_COLS = 3 * LANES


def _inproj_o_kernel(x_ref, g_ref, w_ref, qg_ref, wuq_ref, kvg_ref, wukv_ref, cos_ref, sin_ref,
                     bg_ref, ch_ref, q_ref, k_ref, v_ref, *, qscale):
    h = _rms(x_ref[...], g_ref[...]).astype(BF16)
    for c in range(0, D_C, 512):
        bg_ref[:, c:c + 512] = _dot(h, w_ref[:, c:c + 512])
        ch_ref[:, c:c + 512] = (_dot(h, w_ref[:, D_C + c:D_C + c + 512])
                                * _dot(h, w_ref[:, 2 * D_C + c:2 * D_C + c + 512]))
    cosv = cos_ref[...]
    sinv = sin_ref[...]
    tail = lax.broadcasted_iota(jnp.int32, (V_ROWS - V_HEAD, x_ref.shape[0]), 0)
    ones_row = jnp.where(tail == 0, 1.0, 0.0).astype(BF16)
    lat = _dot(h, w_ref[:, 3 * D_C:])
    o = Q_LORA + KV_LORA
    kr = (lat[:, o:o + LANES] * cosv + lat[:, o + LANES:o + 2 * LANES] * sinv).astype(BF16)
    cq = _rms(lat[:, :Q_LORA], qg_ref[...]).astype(BF16)
    ckv = _rms(lat[:, Q_LORA:o], kvg_ref[...]).astype(BF16)
    for hd in range(MLA_HEADS):
        qh = _dot(cq, wuq_ref[:, hd * Q_COLS:(hd + 1) * Q_COLS])
        q_ref[hd, 0:LANES, :] = (qh[:, 0:LANES] * qscale).T.astype(BF16)
        q_ref[hd, LANES:, :] = ((qh[:, LANES:2 * LANES] * cosv + qh[:, 2 * LANES:] * sinv)
                                * qscale).T.astype(BF16)
        kv = _dot(ckv, wukv_ref[:, hd * 2 * LANES:(hd + 1) * 2 * LANES])
        k_ref[hd, :, 0:LANES] = kv[:, 0:LANES].astype(BF16)
        k_ref[hd, :, LANES:] = kr
        v_ref[hd, 0, 0:V_HEAD, :] = kv[:, LANES:].T.astype(BF16)
        v_ref[hd, 0, V_HEAD:, :] = ones_row


def _inproj_o(x, g, w, qg, wuq, kvg, wukv, cos_t, sin_t, qscale):
    tm = TM_INO
    row = lambda i: (i, 0)
    head = lambda i: (0, i, 0)
    return pl.pallas_call(
        functools.partial(_inproj_o_kernel, qscale=qscale),
        grid=(SEQ // tm,),
        in_specs=[pl.BlockSpec((tm, D_MODEL), row),
                  _resident((1, D_MODEL)),
                  _resident(w.shape),
                  _resident((1, Q_LORA)), _resident(wuq.shape),
                  _resident((1, KV_LORA)), _resident(wukv.shape),
                  pl.BlockSpec((tm, LANES), row), pl.BlockSpec((tm, LANES), row)],
        out_specs=[pl.BlockSpec((tm, D_C), row), pl.BlockSpec((tm, D_C), row),
                   pl.BlockSpec((MLA_HEADS, QK_PAD, tm), lambda i: (0, 0, i)),
                   pl.BlockSpec((MLA_HEADS, tm, QK_PAD), head),
                   pl.BlockSpec((MLA_HEADS, 1, V_ROWS, tm),
                                lambda i: (0, i // (TK_ATT // tm), 0, i % (TK_ATT // tm)))],
        out_shape=[jax.ShapeDtypeStruct((SEQ, D_C), F32), jax.ShapeDtypeStruct((SEQ, D_C), F32),
                   jax.ShapeDtypeStruct((MLA_HEADS, QK_PAD, SEQ), BF16),
                   jax.ShapeDtypeStruct((MLA_HEADS, SEQ, QK_PAD), BF16),
                   jax.ShapeDtypeStruct((MLA_HEADS, SEQ // TK_ATT, V_ROWS, TK_ATT), BF16)],
        compiler_params=_params(("arbitrary",)),
        name="inproj_o",
    )(x, g, w, qg, wuq, kvg, wukv, cos_t, sin_t)


TS_SC = 256
HALO_SC = 8


def _shortconv_tile(t, n, prev_ref, main_ref, next_ref, bg_ref, w_ref, o_ref, ext_ref):
    ext_ref[0:HALO_SC, :] = jnp.where(t > 0, prev_ref[...], 0.0)
    ext_ref[HALO_SC:HALO_SC + TS_SC, :] = main_ref[...]
    ext_ref[HALO_SC + TS_SC:, :] = jnp.where(t < n - 1, next_ref[...], 0.0)
    for r0 in range(0, TS_SC, 128):
        for c in range(0, D_C, LANES):
            cs = slice(c, c + LANES)
            acc = w_ref[0:1, cs] * ext_ref[HALO_SC - 1 + r0:HALO_SC - 1 + r0 + 128, cs]
            acc = acc + w_ref[1:2, cs] * ext_ref[HALO_SC + r0:HALO_SC + r0 + 128, cs]
            acc = acc + w_ref[2:3, cs] * ext_ref[HALO_SC + 1 + r0:HALO_SC + 1 + r0 + 128, cs]
            o_ref[r0:r0 + 128, cs] = (bg_ref[r0:r0 + 128, cs] * acc).astype(BF16)


def _shortconv_specs(tile_of):
    r = TS_SC // HALO_SC
    nh = SEQ // HALO_SC
    ins = [pl.BlockSpec((HALO_SC, D_C), lambda *g: (jnp.maximum(tile_of(*g) * r - 1, 0), 0)),
           pl.BlockSpec((TS_SC, D_C), lambda *g: (tile_of(*g), 0)),
           pl.BlockSpec((HALO_SC, D_C), lambda *g: (jnp.minimum((tile_of(*g) + 1) * r, nh - 1), 0)),
           pl.BlockSpec((TS_SC, D_C), lambda *g: (tile_of(*g), 0)),
           _resident((3, D_C))]
    return ins, pl.BlockSpec((TS_SC, D_C), lambda *g: (tile_of(*g), 0))


TQ_ATT = 2048
TK_ATT = 512
NKB_ATT = SEQ // TK_ATT
LW_ATT = 256
UNROLL_ATT = 8


def _attn_kernel(q_ref, k_ref, v_ref, cp_ref, cm_ref, cn_ref, bg_ref, cw_ref, o_ref, yc_ref,
                 s0, s1, p0, p1, a0, a1, mb0, mb1, m_ref, acc_ref, ext_ref):
    nq = pl.num_programs(1)
    _shortconv_tile(pl.program_id(0) * nq + pl.program_id(1), pl.num_programs(0) * nq,
                    cp_ref, cm_ref, cn_ref, bg_ref, cw_ref, yc_ref, ext_ref)

    chunks = [slice(c, c + LW_ATT) for c in range(0, TQ_ATT, LW_ATT)]

    def scores(kb, s_ref, mb_ref, sl):
        off = pl.multiple_of(kb * TK_ATT, TK_ATT)
        st = _dot(k_ref[0, pl.ds(off, TK_ATT), :], q_ref[0, :, sl])
        s_ref[:, sl] = st
        mb_ref[:, sl] = jnp.max(st, axis=0, keepdims=True)

    def softmax(s_ref, mb_ref, p_ref, a_ref, sl):
        m_prev = m_ref[:, sl]
        m_new = jnp.maximum(m_prev, mb_ref[:, sl])
        a_ref[:, sl] = jnp.exp2(m_prev - m_new)
        m_ref[:, sl] = m_new
        p_ref[:, sl] = jnp.exp2(s_ref[:, sl] - m_new).astype(BF16)

    def values(kb, p_ref, a_ref, sl):
        acc_ref[:, sl] = a_ref[:, sl] * acc_ref[:, sl] + _dot(v_ref[0, kb], p_ref[:, sl])

    def stage(kb_scores, s_w, mb_w, s_r, mb_r, p_w, a_w, kb_values, p_r, a_r):
        for sl in chunks:
            scores(kb_scores, s_w, mb_w, sl)
            softmax(s_r, mb_r, p_w, a_w, sl)
            values(kb_values, p_r, a_r, sl)

    m_ref[...] = jnp.full(m_ref.shape, -jnp.inf, F32)
    acc_ref[...] = jnp.zeros(acc_ref.shape, F32)
    p1[...] = jnp.zeros(p1.shape, BF16)
    a1[...] = jnp.ones(a1.shape, F32)
    for sl in chunks:
        scores(0, s0, mb0, sl)

    def body(i, carry):
        for u in range(0, UNROLL_ATT, 2):
            j = UNROLL_ATT * i + u
            stage(j + 1, s1, mb1, s0, mb0, p0, a0, jnp.maximum(j - 1, 0), p1, a1)
            stage(jnp.minimum(j + 2, NKB_ATT - 1), s0, mb0, s1, mb1, p1, a1, j, p0, a0)
        return carry

    lax.fori_loop(0, NKB_ATT // UNROLL_ATT, body, 0)
    for sl in chunks:
        values(NKB_ATT - 1, p1, a1, sl)
    o_ref[...] = (acc_ref[0:V_HEAD, :] / acc_ref[V_HEAD:V_HEAD + 1, :]).T.astype(BF16)


def _attention(qt, k, vt, bg, ch, conv_w):
    nq = SEQ // TQ_ATT
    assert MLA_HEADS * nq == SEQ // TS_SC
    sc_in, sc_out = _shortconv_specs(lambda h, i: h * nq + i)
    return pl.pallas_call(
        _attn_kernel,
        grid=(MLA_HEADS, nq),
        in_specs=[pl.BlockSpec((1, QK_PAD, TQ_ATT), lambda h, i: (h, 0, i)),
                  pl.BlockSpec((1, SEQ, QK_PAD), lambda h, i: (h, 0, 0)),
                  pl.BlockSpec((1, SEQ // TK_ATT, V_ROWS, TK_ATT), lambda h, i: (h, 0, 0, 0))] + sc_in,
        out_specs=[pl.BlockSpec((TQ_ATT, V_HEAD), lambda h, i: (i, h)), sc_out],
        out_shape=[jax.ShapeDtypeStruct((SEQ, D_ATT), BF16), jax.ShapeDtypeStruct((SEQ, D_C), BF16)],
        scratch_shapes=[pltpu.VMEM((TK_ATT, TQ_ATT), F32), pltpu.VMEM((TK_ATT, TQ_ATT), F32),
                        pltpu.VMEM((TK_ATT, TQ_ATT), BF16), pltpu.VMEM((TK_ATT, TQ_ATT), BF16),
                        pltpu.VMEM((1, TQ_ATT), F32), pltpu.VMEM((1, TQ_ATT), F32),
                        pltpu.VMEM((1, TQ_ATT), F32), pltpu.VMEM((1, TQ_ATT), F32),
                        pltpu.VMEM((1, TQ_ATT), F32),
                        pltpu.VMEM((V_ROWS, TQ_ATT), F32),
                        pltpu.VMEM((TS_SC + 2 * HALO_SC, D_C), F32)],
        compiler_params=_params(("arbitrary", "arbitrary")),
        name="mla_attention",
    )(qt, k, vt, ch, ch, ch, bg, conv_w)


def _rope_tiles(w_rope):
    half = QK_ROPE // 2
    z = jnp.zeros((w_rope.shape[0], LANES - QK_ROPE), w_rope.dtype)
    a = jnp.concatenate([w_rope, z], axis=1)
    b = jnp.concatenate([w_rope[:, half:], w_rope[:, :half], z], axis=1)
    return a, b


def _rope_tables():
    inv = 1.0 / (ROPE_THETA ** (jnp.arange(0, QK_ROPE, 2, dtype=F32) / QK_ROPE))
    hi = (float(LANES) * jnp.arange(SEQ // LANES, dtype=F32))[:, None] * inv[None, :]
    lo = jnp.arange(LANES, dtype=F32)[:, None] * inv[None, :]
    ch, sh, cl, sl = jnp.cos(hi)[:, None], jnp.sin(hi)[:, None], jnp.cos(lo)[None], jnp.sin(lo)[None]
    cos = (ch * cl - sh * sl).reshape(SEQ, QK_ROPE // 2)
    sin = (sh * cl + ch * sl).reshape(SEQ, QK_ROPE // 2)
    z = jnp.zeros((SEQ, LANES - QK_ROPE), F32)
    return (jnp.concatenate([cos, cos, z], axis=1), jnp.concatenate([-sin, sin, z], axis=1))


def kernel(x, mix_norm_e, w_in_e, conv_a_w, conv_a_b, ln_a_g, ln_a_b, w_out_e, mix_norm_o, w_in_o,
           conv_c_w, q_norm_g, w_uq, kv_norm_g, w_ukv, w_out_o, mlp_norm, w_up, w_down, final_norm):
    xs = x[0]
    row = lambda v: v.reshape(1, -1)

    hglu, uf = _inproj_e(xs, row(mix_norm_e[0]), w_in_e[0].astype(BF16))
    ya = _conformer(hglu, conv_a_w[0], row(conv_a_b[0]), row(ln_a_g[0]), row(ln_a_b[0]))
    yb = _fourier_mix(uf)
    w_up_b = w_up.astype(BF16)
    w_down_b = w_down.astype(BF16)
    xs = _outproj_mlp(xs, ya, yb, w_out_e[0].astype(BF16), row(mlp_norm[0]),
                      w_up_b, w_down_b, row(final_norm), layer=0, final=False)

    wi = w_in_o[0]
    o = 3 * D_C + Q_LORA + KV_LORA
    kra, krb = _rope_tiles(wi[:, o:])
    w_in = jnp.concatenate([wi[:, :o], kra, krb], axis=1).astype(BF16)
    wq = w_uq[0].reshape(Q_LORA, MLA_HEADS, QK_HEAD)
    qa, qb = _rope_tiles(wq[:, :, QK_NOPE:].reshape(Q_LORA * MLA_HEADS, QK_ROPE))
    wuq = jnp.concatenate([wq[:, :, :QK_NOPE], qa.reshape(Q_LORA, MLA_HEADS, LANES),
                           qb.reshape(Q_LORA, MLA_HEADS, LANES)], axis=2)
    wuq = wuq.reshape(Q_LORA, MLA_HEADS * Q_COLS).astype(BF16)
    cos_t, sin_t = _rope_tables()
    qscale = (QK_HEAD ** -0.5) * math.log2(math.e)
    bg, ch, q, k, v = _inproj_o(xs, row(mix_norm_o[0]), w_in,
                                row(q_norm_g[0]), wuq, row(kv_norm_g[0]), w_ukv[0].astype(BF16),
                                cos_t, sin_t, qscale)
    yd, yc = _attention(q, k, v, bg, ch, conv_c_w[0])
    xs = _outproj_mlp(xs, yc, yd, w_out_o[0].astype(BF16), row(mlp_norm[1]),
                      w_up_b, w_down_b, row(final_norm), layer=1, final=True)
    return xs[None]
```

```python
import functools
import math

import numpy as np
import jax
import jax.numpy as jnp
from jax import lax
from jax.experimental import pallas as pl
from jax.experimental.pallas import tpu as pltpu

D_MODEL = 2048
SEQ = 16384
D_A = 1024
CONF_KERNEL = 31
D_B = 1024
FNET_GROUPS = 8
FNET_GROUP_CH = 128
D_C = 1024
MLA_HEADS = 8
Q_LORA = 512
KV_LORA = 256
QK_NOPE = 128
QK_ROPE = 64
V_HEAD = 128
QK_HEAD = QK_NOPE + QK_ROPE
D_ATT = MLA_HEADS * V_HEAD
ROPE_THETA = 10000.0
D_FF = 4 * D_MODEL
EPS = 1e-6

LANES = 128
QK_PAD = 2 * LANES
V_ROWS = V_HEAD + 16
FFT_N1 = 128
FFT_N2 = 128
VMEM_LIMIT = 56 * 1024 * 1024

BF16 = jnp.bfloat16
F32 = jnp.float32


def _dot(a, b):
    return jnp.dot(a, b, preferred_element_type=F32)


def _rms(x, g):
    return x * lax.rsqrt(jnp.mean(x * x, axis=-1, keepdims=True) + EPS) * g


def _params(sem):
    return pltpu.CompilerParams(dimension_semantics=sem, vmem_limit_bytes=VMEM_LIMIT)


def _resident(shape):
    nd = len(shape)
    return pl.BlockSpec(shape, lambda *_: (0,) * nd, pipeline_mode=pl.Buffered(1))


TM_IN = 512


def _inproj_e_kernel(x_ref, g_ref, w_ref, glu_ref, uf_ref):
    h = _rms(x_ref[...], g_ref[...]).astype(BF16)
    for c in range(0, D_A, 512):
        val = _dot(h, w_ref[:, c:c + 512])
        gate = _dot(h, w_ref[:, D_A + c:D_A + c + 512])
        glu_ref[:, c:c + 512] = val * jax.nn.sigmoid(gate)
    for c in range(0, D_B, 512):
        uf_ref[:, c:c + 512] = _dot(h, w_ref[:, 2 * D_A + c:2 * D_A + c + 512]).astype(BF16)


def _inproj_e(x, g, w):
    return pl.pallas_call(
        _inproj_e_kernel,
        grid=(SEQ // TM_IN,),
        in_specs=[pl.BlockSpec((TM_IN, D_MODEL), lambda i: (i, 0)),
                  _resident((1, D_MODEL)),
                  _resident((D_MODEL, 2 * D_A + D_B))],
        out_specs=[pl.BlockSpec((TM_IN, D_A), lambda i: (i, 0)),
                   pl.BlockSpec((TM_IN, D_B), lambda i: (i, 0))],
        out_shape=[jax.ShapeDtypeStruct((SEQ, D_A), F32),
                   jax.ShapeDtypeStruct((SEQ, D_B), BF16)],
        compiler_params=_params(("arbitrary",)),
        name="inproj_e",
    )(x, g, w)


TC_CONF = 128
HALO_CONF = 16


SUBLANES = 8
CONF_Q = (-2, -1, 0, 1, 2)


def _conf_tap_plan():
    plan = np.full((len(CONF_Q), SUBLANES, SUBLANES), -1, np.int32)
    for qi, q in enumerate(CONF_Q):
        for r in range(SUBLANES):
            for i in range(SUBLANES):
                d = 8 * q + r if i + r < SUBLANES else 8 * (q - 1) + r
                if abs(d) <= CONF_KERNEL // 2:
                    plan[qi, r, i] = d + CONF_KERNEL // 2
    return plan


def _conformer_kernel(prev_ref, main_ref, next_ref, w_ref, b_ref, lg_ref, lb_ref, o_ref,
                      hext_ref, conv_ref, *, live):
    i = pl.program_id(0)
    n = pl.num_programs(0)
    hext_ref[0:HALO_CONF, :] = jnp.where(i > 0, prev_ref[...], 0.0)
    hext_ref[HALO_CONF:HALO_CONF + TC_CONF, :] = main_ref[...]
    hext_ref[HALO_CONF + TC_CONF:, :] = jnp.where(i < n - 1, next_ref[...], 0.0)
    nt_out = TC_CONF // SUBLANES
    nt_ext = (TC_CONF + 2 * HALO_CONF) // SUBLANES
    t0 = HALO_CONF // SUBLANES
    for c in range(0, D_A, LANES):
        x3 = hext_ref[:, c:c + LANES].reshape(nt_ext, SUBLANES, LANES)
        acc = jnp.zeros((nt_out, SUBLANES, LANES), F32) + b_ref[:, c:c + LANES]
        for r in range(SUBLANES):
            xr = x3 if r == 0 else pltpu.roll(x3, SUBLANES - r, axis=1)
            for qi, q in enumerate(CONF_Q):
                if live[qi][r]:
                    row = (qi * SUBLANES + r) * SUBLANES
                    acc = acc + w_ref[row:row + SUBLANES, c:c + LANES] * xr[t0 + q:t0 + q + nt_out]
        conv_ref[:, c:c + LANES] = acc.reshape(TC_CONF, LANES)
    y = conv_ref[...]
    mu = jnp.mean(y, axis=-1, keepdims=True)
    d = y - mu
    var = jnp.mean(d * d, axis=-1, keepdims=True)
    z = d * lax.rsqrt(var + EPS) * lg_ref[...] + lb_ref[...]
    o_ref[...] = (z * jax.nn.sigmoid(z)).astype(BF16)


def _conformer(hglu, w, b, lg, lb):
    r = TC_CONF // HALO_CONF
    nh = SEQ // HALO_CONF
    plan = _conf_tap_plan()
    live = tuple(tuple(bool((plan[qi, rr] >= 0).any()) for rr in range(SUBLANES))
                 for qi in range(len(CONF_Q)))
    wtab = jnp.where((plan >= 0)[..., None], w[np.maximum(plan, 0)], 0.0).reshape(-1, D_A)
    return pl.pallas_call(
        functools.partial(_conformer_kernel, live=live),
        grid=(SEQ // TC_CONF,),
        in_specs=[pl.BlockSpec((HALO_CONF, D_A), lambda i: (jnp.maximum(i * r - 1, 0), 0)),
                  pl.BlockSpec((TC_CONF, D_A), lambda i: (i, 0)),
                  pl.BlockSpec((HALO_CONF, D_A), lambda i: (jnp.minimum((i + 1) * r, nh - 1), 0)),
                  _resident((len(CONF_Q) * SUBLANES * SUBLANES, D_A)),
                  _resident((1, D_A)), _resident((1, D_A)), _resident((1, D_A))],
        out_specs=pl.BlockSpec((TC_CONF, D_A), lambda i: (i, 0)),
        out_shape=jax.ShapeDtypeStruct((SEQ, D_A), BF16),
        scratch_shapes=[pltpu.VMEM((TC_CONF + 2 * HALO_CONF, D_A), F32),
                        pltpu.VMEM((TC_CONF, D_A), F32)],
        compiler_params=_params(("arbitrary",)),
        name="conformer",
    )(hglu, hglu, hglu, wtab, b, lg, lb)


TN_FFT1 = 8192
TK1_FFT2 = 8


def _fft1_kernel(x_ref, f_ref, tr_ref, ti_ref):
    for c in range(0, TN_FFT1, 1024):
        t = _dot(f_ref[...], x_ref[:, c:c + 1024])
        tr_ref[:, c:c + 1024] = t[:FFT_N1].astype(BF16)
        ti_ref[:, c:c + 1024] = t[FFT_N1:].astype(BF16)


def _fft1(x2d, f_stack):
    ncol = FFT_N2 * D_B
    return pl.pallas_call(
        _fft1_kernel,
        grid=(ncol // TN_FFT1,),
        in_specs=[pl.BlockSpec((FFT_N1, TN_FFT1), lambda j: (0, j)),
                  _resident((2 * FFT_N1, FFT_N1))],
        out_specs=[pl.BlockSpec((FFT_N1, TN_FFT1), lambda j: (0, j)),
                   pl.BlockSpec((FFT_N1, TN_FFT1), lambda j: (0, j))],
        out_shape=[jax.ShapeDtypeStruct((FFT_N1, ncol), BF16),
                   jax.ShapeDtypeStruct((FFT_N1, ncol), BF16)],
        compiler_params=_params(("arbitrary",)),
        name="fft_stage1",
    )(x2d, f_stack)


def _fft2_kernel(tr_ref, ti_ref, m_ref, cs_ref, o_ref):
    zs = []
    for j in range(TK1_FFT2):
        t = jnp.concatenate([tr_ref[j], ti_ref[j]], axis=0)
        z = _dot(m_ref[j], t)
        zs.append((z[:FFT_N2].astype(BF16), z[FFT_N2:].astype(BF16)))
    for g in range(FNET_GROUPS):
        lo = g * FNET_GROUP_CH
        zz = jnp.concatenate(
            [jnp.concatenate([zr[:, lo:lo + FNET_GROUP_CH], zi[:, lo:lo + FNET_GROUP_CH]], axis=1)
             for zr, zi in zs], axis=0)
        y = _dot(zz, cs_ref[...]).astype(BF16)
        for j in range(TK1_FFT2):
            o_ref[:, j * D_B + lo:j * D_B + lo + FNET_GROUP_CH] = y[j * FFT_N2:(j + 1) * FFT_N2]


def _fft2(tr3, ti3, m_full, cs):
    return pl.pallas_call(
        _fft2_kernel,
        grid=(FFT_N1 // TK1_FFT2,),
        in_specs=[pl.BlockSpec((TK1_FFT2, FFT_N2, D_B), lambda i: (i, 0, 0)),
                  pl.BlockSpec((TK1_FFT2, FFT_N2, D_B), lambda i: (i, 0, 0)),
                  pl.BlockSpec((TK1_FFT2, 2 * FFT_N2, 2 * FFT_N2), lambda i: (i, 0, 0)),
                  _resident((2 * FNET_GROUP_CH, FNET_GROUP_CH))],
        out_specs=pl.BlockSpec((FFT_N2, TK1_FFT2 * D_B), lambda i: (0, i)),
        out_shape=jax.ShapeDtypeStruct((FFT_N2, FFT_N1 * D_B), BF16),
        compiler_params=_params(("arbitrary",)),
        name="fft_stage2",
    )(tr3, ti3, m_full, cs)


def _dft_tables():
    n = np.arange(128)
    ph = 2.0 * np.pi * ((n[:, None] * n[None, :]) % 128) / 128.0
    fr, fi = np.cos(ph), -np.sin(ph)
    f_stack = np.concatenate([fr, fi], axis=0) / math.sqrt(SEQ)
    tw_ph = 2.0 * np.pi * (n[:, None] * n[None, :]) / SEQ
    twr, twi = np.cos(tw_ph), -np.sin(tw_ph)
    cs = np.concatenate([np.cos(ph), np.sin(ph)], axis=0) / math.sqrt(FNET_GROUP_CH)
    return (jnp.asarray(f_stack, F32).astype(BF16), jnp.asarray(fr, F32), jnp.asarray(fi, F32),
            jnp.asarray(twr, F32), jnp.asarray(twi, F32), jnp.asarray(cs, F32).astype(BF16))


def _fourier_mix(uf):
    f_stack, fr, fi, twr, twi, cs = _dft_tables()
    mr = fr[None] * twr[:, None, :] - fi[None] * twi[:, None, :]
    mi = fr[None] * twi[:, None, :] + fi[None] * twr[:, None, :]
    m_full = jnp.concatenate([jnp.concatenate([mr, -mi], axis=2),
                              jnp.concatenate([mi, mr], axis=2)], axis=1).astype(BF16)
    tr, ti = _fft1(uf.reshape(FFT_N1, FFT_N2 * D_B), f_stack)
    y2d = _fft2(tr.reshape(FFT_N1, FFT_N2, D_B), ti.reshape(FFT_N1, FFT_N2, D_B), m_full, cs)
    return y2d.reshape(SEQ, D_B)


TM_MLP = 512
TF_MLP = 1024


def _mlp_kernel(x_ref, ya_ref, yb_ref, wo_ref, g_ref, wup_ref, wdn_ref, gf_ref, o_ref, h_ref, *,
                final):
    f = pl.program_id(1)
    half = wo_ref.shape[0] // 2

    @pl.when(f == 0)
    def _():
        for c in range(0, D_MODEL, 512):
            o_ref[:, c:c + 512] = (x_ref[:, c:c + 512]
                                   + _dot(ya_ref[...], wo_ref[0:half, c:c + 512])
                                   + _dot(yb_ref[...], wo_ref[half:, c:c + 512]))
        h_ref[...] = _rms(o_ref[...], g_ref[...]).astype(BF16)

    a = _dot(h_ref[...], wup_ref[...])
    a = jnp.square(jnp.maximum(a, 0.0)).astype(BF16)
    for c in range(0, D_MODEL, 512):
        o_ref[:, c:c + 512] += _dot(a, wdn_ref[:, c:c + 512])

    if final:
        @pl.when(f == pl.num_programs(1) - 1)
        def _():
            o_ref[...] = _rms(o_ref[...], gf_ref[...])


def _outproj_mlp(x, ya, yb, wo, g, wup, wdn, gf, layer, final):
    return pl.pallas_call(
        functools.partial(_mlp_kernel, final=final),
        grid=(SEQ // TM_MLP, D_FF // TF_MLP),
        in_specs=[pl.BlockSpec((TM_MLP, D_MODEL), lambda i, f: (i, 0)),
                  pl.BlockSpec((TM_MLP, ya.shape[1]), lambda i, f: (i, 0)),
                  pl.BlockSpec((TM_MLP, yb.shape[1]), lambda i, f: (i, 0)),
                  _resident(wo.shape),
                  _resident((1, D_MODEL)),
                  pl.BlockSpec((None, D_MODEL, TF_MLP), lambda i, f: (layer, 0, f)),
                  pl.BlockSpec((None, TF_MLP, D_MODEL), lambda i, f: (layer, f, 0)),
                  _resident((1, D_MODEL))],
        out_specs=pl.BlockSpec((TM_MLP, D_MODEL), lambda i, f: (i, 0)),
        out_shape=jax.ShapeDtypeStruct((SEQ, D_MODEL), F32),
        scratch_shapes=[pltpu.VMEM((TM_MLP, D_MODEL), BF16)],
        compiler_params=_params(("arbitrary", "arbitrary")),
        name="outproj_mlp_final" if final else "outproj_mlp",
    )(x, ya, yb, wo, g, wup, wdn, gf)


TM_INO = 512
Q_COLS = 3 * LANES


def _inproj_o_kernel(x_ref, g_ref, w_ref, qg_ref, wuq_ref, kvg_ref, wukv_ref, cos_ref, sin_ref,
                     bg_ref, ch_ref, q_ref, k_ref, v_ref, *, qscale):
    h = _rms(x_ref[...], g_ref[...]).astype(BF16)
    for c in range(0, D_C, 512):
        bg_ref[:, c:c + 512] = _dot(h, w_ref[:, c:c + 512])
        ch_ref[:, c:c + 512] = (_dot(h, w_ref[:, D_C + c:D_C + c + 512])
                                * _dot(h, w_ref[:, 2 * D_C + c:2 * D_C + c + 512]))
    cosv = cos_ref[...]
    sinv = sin_ref[...]
    tail = lax.broadcasted_iota(jnp.int32, (V_ROWS - V_HEAD, x_ref.shape[0]), 0)
    ones_row = jnp.where(tail == 0, 1.0, 0.0).astype(BF16)
    lat = _dot(h, w_ref[:, 3 * D_C:])
    o = Q_LORA + KV_LORA
    kr = (lat[:, o:o + LANES] * cosv + lat[:, o + LANES:o + 2 * LANES] * sinv).astype(BF16)
    cq = _rms(lat[:, :Q_LORA], qg_ref[...]).astype(BF16)
    ckv = _rms(lat[:, Q_LORA:o], kvg_ref[...]).astype(BF16)
    for hd in range(MLA_HEADS):
        qh = _dot(cq, wuq_ref[:, hd * Q_COLS:(hd + 1) * Q_COLS])
        q_ref[hd, 0:LANES, :] = (qh[:, 0:LANES] * qscale).T.astype(BF16)
        q_ref[hd, LANES:, :] = ((qh[:, LANES:2 * LANES] * cosv + qh[:, 2 * LANES:] * sinv)
                                * qscale).T.astype(BF16)
        kv = _dot(ckv, wukv_ref[:, hd * 2 * LANES:(hd + 1) * 2 * LANES])
        k_ref[hd, :, 0:LANES] = kv[:, 0:LANES].astype(BF16)
        k_ref[hd, :, LANES:] = kr
        v_ref[hd, 0, 0:V_HEAD, :] = kv[:, LANES:].T.astype(BF16)
        v_ref[hd, 0, V_HEAD:, :] = ones_row


def _inproj_o(x, g, w, qg, wuq, kvg, wukv, cos_t, sin_t, qscale):
    tm = TM_INO
    row = lambda i: (i, 0)
    head = lambda i: (0, i, 0)
    return pl.pallas_call(
        functools.partial(_inproj_o_kernel, qscale=qscale),
        grid=(SEQ // tm,),
        in_specs=[pl.BlockSpec((tm, D_MODEL), row),
                  _resident((1, D_MODEL)),
                  _resident(w.shape),
                  _resident((1, Q_LORA)), _resident(wuq.shape),
                  _resident((1, KV_LORA)), _resident(wukv.shape),
                  pl.BlockSpec((tm, LANES), row), pl.BlockSpec((tm, LANES), row)],
        out_specs=[pl.BlockSpec((tm, D_C), row), pl.BlockSpec((tm, D_C), row),
                   pl.BlockSpec((MLA_HEADS, QK_PAD, tm), lambda i: (0, 0, i)),
                   pl.BlockSpec((MLA_HEADS, tm, QK_PAD), head),
                   pl.BlockSpec((MLA_HEADS, 1, V_ROWS, tm),
                                lambda i: (0, i // (TK_ATT // tm), 0, i % (TK_ATT // tm)))],
        out_shape=[jax.ShapeDtypeStruct((SEQ, D_C), F32), jax.ShapeDtypeStruct((SEQ, D_C), F32),
                   jax.ShapeDtypeStruct((MLA_HEADS, QK_PAD, SEQ), BF16),
                   jax.ShapeDtypeStruct((MLA_HEADS, SEQ, QK_PAD), BF16),
                   jax.ShapeDtypeStruct((MLA_HEADS, SEQ // TK_ATT, V_ROWS, TK_ATT), BF16)],
        compiler_params=_params(("arbitrary",)),
        name="inproj_o",
    )(x, g, w, qg, wuq, kvg, wukv, cos_t, sin_t)


TS_SC = 256
HALO_SC = 8


def _shortconv_tile(t, n, prev_ref, main_ref, next_ref, bg_ref, w_ref, o_ref, ext_ref):
    ext_ref[0:HALO_SC, :] = jnp.where(t > 0, prev_ref[...], 0.0)
    ext_ref[HALO_SC:HALO_SC + TS_SC, :] = main_ref[...]
    ext_ref[HALO_SC + TS_SC:, :] = jnp.where(t < n - 1, next_ref[...], 0.0)
    for r0 in range(0, TS_SC, 128):
        for c in range(0, D_C, LANES):
            cs = slice(c, c + LANES)
            acc = w_ref[0:1, cs] * ext_ref[HALO_SC - 1 + r0:HALO_SC - 1 + r0 + 128, cs]
            acc = acc + w_ref[1:2, cs] * ext_ref[HALO_SC + r0:HALO_SC + r0 + 128, cs]
            acc = acc + w_ref[2:3, cs] * ext_ref[HALO_SC + 1 + r0:HALO_SC + 1 + r0 + 128, cs]
            o_ref[r0:r0 + 128, cs] = (bg_ref[r0:r0 + 128, cs] * acc).astype(BF16)


def _shortconv_specs(tile_of):
    r = TS_SC // HALO_SC
    nh = SEQ // HALO_SC
    ins = [pl.BlockSpec((HALO_SC, D_C), lambda *g: (jnp.maximum(tile_of(*g) * r - 1, 0), 0)),
           pl.BlockSpec((TS_SC, D_C), lambda *g: (tile_of(*g), 0)),
           pl.BlockSpec((HALO_SC, D_C), lambda *g: (jnp.minimum((tile_of(*g) + 1) * r, nh - 1), 0)),
           pl.BlockSpec((TS_SC, D_C), lambda *g: (tile_of(*g), 0)),
           _resident((3, D_C))]
    return ins, pl.BlockSpec((TS_SC, D_C), lambda *g: (tile_of(*g), 0))


TQ_ATT = 2048
TK_ATT = 512
NKB_ATT = SEQ // TK_ATT
LW_ATT = 256
UNROLL_ATT = 8


def _attn_kernel(q_ref, k_ref, v_ref, cp_ref, cm_ref, cn_ref, bg_ref, cw_ref, o_ref, yc_ref,
                 s0, s1, p0, p1, a0, a1, mb0, mb1, m_ref, acc_ref, ext_ref):
    nq = pl.num_programs(1)
    _shortconv_tile(pl.program_id(0) * nq + pl.program_id(1), pl.num_programs(0) * nq,
                    cp_ref, cm_ref, cn_ref, bg_ref, cw_ref, yc_ref, ext_ref)

    chunks = [slice(c, c + LW_ATT) for c in range(0, TQ_ATT, LW_ATT)]

    def scores(kb, s_ref, mb_ref, sl):
        off = pl.multiple_of(kb * TK_ATT, TK_ATT)
        st = _dot(k_ref[0, pl.ds(off, TK_ATT), :], q_ref[0, :, sl])
        s_ref[:, sl] = st
        mb_ref[:, sl] = jnp.max(st, axis=0, keepdims=True)

    def softmax(s_ref, mb_ref, p_ref, a_ref, sl):
        m_prev = m_ref[:, sl]
        m_new = jnp.maximum(m_prev, mb_ref[:, sl])
        a_ref[:, sl] = jnp.exp2(m_prev - m_new)
        m_ref[:, sl] = m_new
        p_ref[:, sl] = jnp.exp2(s_ref[:, sl] - m_new).astype(BF16)

    def values(kb, p_ref, a_ref, sl):
        acc_ref[:, sl] = a_ref[:, sl] * acc_ref[:, sl] + _dot(v_ref[0, kb], p_ref[:, sl])

    def stage(kb_scores, s_w, mb_w, s_r, mb_r, p_w, a_w, kb_values, p_r, a_r):
        for sl in chunks:
            scores(kb_scores, s_w, mb_w, sl)
            softmax(s_r, mb_r, p_w, a_w, sl)
            values(kb_values, p_r, a_r, sl)

    m_ref[...] = jnp.full(m_ref.shape, -jnp.inf, F32)
    acc_ref[...] = jnp.zeros(acc_ref.shape, F32)
    p1[...] = jnp.zeros(p1.shape, BF16)
    a1[...] = jnp.ones(a1.shape, F32)
    for sl in chunks:
        scores(0, s0, mb0, sl)

    def body(i, carry):
        for u in range(0, UNROLL_ATT, 2):
            j = UNROLL_ATT * i + u
            stage(j + 1, s1, mb1, s0, mb0, p0, a0, jnp.maximum(j - 1, 0), p1, a1)
            stage(jnp.minimum(j + 2, NKB_ATT - 1), s0, mb0, s1, mb1, p1, a1, j, p0, a0)
        return carry

    lax.fori_loop(0, NKB_ATT // UNROLL_ATT, body, 0)
    for sl in chunks:
        values(NKB_ATT - 1, p1, a1, sl)
    o_ref[...] = (acc_ref[0:V_HEAD, :] / acc_ref[V_HEAD:V_HEAD + 1, :]).T.astype(BF16)


def _attention(qt, k, vt, bg, ch, conv_w):
    nq = SEQ // TQ_ATT
    assert MLA_HEADS * nq == SEQ // TS_SC
    sc_in, sc_out = _shortconv_specs(lambda h, i: h * nq + i)
    return pl.pallas_call(
        _attn_kernel,
        grid=(MLA_HEADS, nq),
        in_specs=[pl.BlockSpec((1, QK_PAD, TQ_ATT), lambda h, i: (h, 0, i)),
                  pl.BlockSpec((1, SEQ, QK_PAD), lambda h, i: (h, 0, 0)),
                  pl.BlockSpec((1, SEQ // TK_ATT, V_ROWS, TK_ATT), lambda h, i: (h, 0, 0, 0))] + sc_in,
        out_specs=[pl.BlockSpec((TQ_ATT, V_HEAD), lambda h, i: (i, h)), sc_out],
        out_shape=[jax.ShapeDtypeStruct((SEQ, D_ATT), BF16), jax.ShapeDtypeStruct((SEQ, D_C), BF16)],
        scratch_shapes=[pltpu.VMEM((TK_ATT, TQ_ATT), F32), pltpu.VMEM((TK_ATT, TQ_ATT), F32),
                        pltpu.VMEM((TK_ATT, TQ_ATT), BF16), pltpu.VMEM((TK_ATT, TQ_ATT), BF16),
                        pltpu.VMEM((1, TQ_ATT), F32), pltpu.VMEM((1, TQ_ATT), F32),
                        pltpu.VMEM((1, TQ_ATT), F32), pltpu.VMEM((1, TQ_ATT), F32),
                        pltpu.VMEM((1, TQ_ATT), F32),
                        pltpu.VMEM((V_ROWS, TQ_ATT), F32),
                        pltpu.VMEM((TS_SC + 2 * HALO_SC, D_C), F32)],
        compiler_params=_params(("arbitrary", "arbitrary")),
        name="mla_attention",
    )(qt, k, vt, ch, ch, ch, bg, conv_w)


def _rope_tiles(w_rope):
    half = QK_ROPE // 2
    z = jnp.zeros((w_rope.shape[0], LANES - QK_ROPE), w_rope.dtype)
    a = jnp.concatenate([w_rope, z], axis=1)
    b = jnp.concatenate([w_rope[:, half:], w_rope[:, :half], z], axis=1)
    return a, b


def _rope_tables():
    inv = 1.0 / (ROPE_THETA ** (jnp.arange(0, QK_ROPE, 2, dtype=F32) / QK_ROPE))
    hi = (float(LANES) * jnp.arange(SEQ // LANES, dtype=F32))[:, None] * inv[None, :]
    lo = jnp.arange(LANES, dtype=F32)[:, None] * inv[None, :]
    ch, sh, cl, sl = jnp.cos(hi)[:, None], jnp.sin(hi)[:, None], jnp.cos(lo)[None], jnp.sin(lo)[None]
    cos = (ch * cl - sh * sl).reshape(SEQ, QK_ROPE // 2)
    sin = (sh * cl + ch * sl).reshape(SEQ, QK_ROPE // 2)
    z = jnp.zeros((SEQ, LANES - QK_ROPE), F32)
    return (jnp.concatenate([cos, cos, z], axis=1), jnp.concatenate([-sin, sin, z], axis=1))


def kernel(x, mix_norm_e, w_in_e, conv_a_w, conv_a_b, ln_a_g, ln_a_b, w_out_e, mix_norm_o, w_in_o,
           conv_c_w, q_norm_g, w_uq, kv_norm_g, w_ukv, w_out_o, mlp_norm, w_up, w_down, final_norm):
    xs = x[0]
    row = lambda v: v.reshape(1, -1)

    hglu, uf = _inproj_e(xs, row(mix_norm_e[0]), w_in_e[0].astype(BF16))
    ya = _conformer(hglu, conv_a_w[0], row(conv_a_b[0]), row(ln_a_g[0]), row(ln_a_b[0]))
    yb = _fourier_mix(uf)
    w_up_b = w_up.astype(BF16)
    w_down_b = w_down.astype(BF16)
    xs = _outproj_mlp(xs, ya, yb, w_out_e[0].astype(BF16), row(mlp_norm[0]),
                      w_up_b, w_down_b, row(final_norm), layer=0, final=False)

    wi = w_in_o[0]
    o = 3 * D_C + Q_LORA + KV_LORA
    kra, krb = _rope_tiles(wi[:, o:])
    w_in = jnp.concatenate([wi[:, :o], kra, krb], axis=1).astype(BF16)
    wq = w_uq[0].reshape(Q_LORA, MLA_HEADS, QK_HEAD)
    qa, qb = _rope_tiles(wq[:, :, QK_NOPE:].reshape(Q_LORA * MLA_HEADS, QK_ROPE))
    wuq = jnp.concatenate([wq[:, :, :QK_NOPE], qa.reshape(Q_LORA, MLA_HEADS, LANES),
                           qb.reshape(Q_LORA, MLA_HEADS, LANES)], axis=2)
    wuq = wuq.reshape(Q_LORA, MLA_HEADS * Q_COLS).astype(BF16)
    cos_t, sin_t = _rope_tables()
    qscale = (QK_HEAD ** -0.5) * math.log2(math.e)
    bg, ch, q, k, v = _inproj_o(xs, row(mix_norm_o[0]), w_in,
                                row(q_norm_g[0]), wuq, row(kv_norm_g[0]), w_ukv[0].astype(BF16),
                                cos_t, sin_t, qscale)
    yd, yc = _attention(q, k, v, bg, ch, conv_c_w[0])
    xs = _outproj_mlp(xs, yc, yd, w_out_o[0].astype(BF16), row(mlp_norm[1]),
                      w_up_b, w_down_b, row(final_norm), layer=1, final=True)
    return xs[None]
```
